```python
import jax, jax.numpy as jnp
from jax import lax
import numpy as np

D_MODEL = 1024
BATCH = 8
SEQ = 2048
DEPTH = 4

GRID_W = 64
HEAD_DIM = 64
N_Q_HEADS = 8
N_KV_HEADS = 2
Q_GROUP = N_Q_HEADS // N_KV_HEADS
ATTN_DIM = N_Q_HEADS * HEAD_DIM
KV_DIM = N_KV_HEADS * HEAD_DIM
Q_BLOCK = 128
ROPE_THETA = 10000.0
ROPE_AXIS_DIM = HEAD_DIM // 2
CONV_HEADS = 8
CONV_DIM = CONV_HEADS * HEAD_DIM
IN_DIM = ATTN_DIM + 2 * KV_DIM + 3 * CONV_DIM
MIX_DIM = ATTN_DIM + CONV_DIM
POOL_WINDOWS = (2, 4, 8, 16)
POOL_GROUP = D_MODEL // len(POOL_WINDOWS)
D_FF = 2816
N_EVEN = (DEPTH + 1) // 2
N_ODD = DEPTH // 2
RMS_EPS = 1e-6

kernel_name = "hybrid_attn_shortconv_pool_convffn_encoder"


def rmsnorm(x, g):
    xf = x.astype(jnp.float32)
    y = xf * lax.rsqrt(jnp.mean(xf * xf, axis=-1, keepdims=True) + RMS_EPS)
    return (y * g.astype(jnp.float32)).astype(x.dtype)


def dwconv3(x, w):
    xp = jnp.pad(x, ((0, 0), (1, 1), (0, 0)))
    return xp[:, :-2] * w[0] + xp[:, 1:-1] * w[1] + xp[:, 2:] * w[2]


def axial_rope_tables(rows):
    row = jnp.repeat(jnp.arange(rows, dtype=jnp.float32), GRID_W)
    col = jnp.tile(jnp.arange(GRID_W, dtype=jnp.float32), rows)
    inv = ROPE_THETA ** (-jnp.arange(0, ROPE_AXIS_DIM, 2, dtype=jnp.float32) / ROPE_AXIS_DIM)
    ang_r = row[:, None] * inv[None, :]
    ang_c = col[:, None] * inv[None, :]
    return jnp.cos(ang_r), jnp.sin(ang_r), jnp.cos(ang_c), jnp.sin(ang_c)


def rotate_axis(x, cos, sin):
    c = cos[None, :, None, :].astype(x.dtype)
    s = sin[None, :, None, :].astype(x.dtype)
    x1, x2 = jnp.split(x, 2, axis=-1)
    return jnp.concatenate([x1 * c - x2 * s, x2 * c + x1 * s], axis=-1)


def apply_axial_rope(x, tables):
    cos_r, sin_r, cos_c, sin_c = tables
    x_row, x_col = jnp.split(x, 2, axis=-1)
    return jnp.concatenate([rotate_axis(x_row, cos_r, sin_r), rotate_axis(x_col, cos_c, sin_c)], axis=-1)


def gqa_block_attention(q, k, v):
    B, S = q.shape[0], q.shape[1]
    nblk = S // Q_BLOCK
    qb = q.reshape(B, nblk, Q_BLOCK, N_KV_HEADS, Q_GROUP, HEAD_DIM).transpose(1, 0, 2, 3, 4, 5)
    scale = HEAD_DIM ** -0.5

    def one_block(q_blk):
        s = jnp.einsum('bqhgd,bkhd->bhgqk', q_blk, k, preferred_element_type=jnp.float32) * scale
        p = jax.nn.softmax(s, axis=-1).astype(v.dtype)
        return jnp.einsum('bhgqk,bkhd->bqhgd', p, v)

    ob = lax.map(one_block, qb)
    return ob.transpose(1, 0, 2, 3, 4, 5).reshape(B, S, ATTN_DIM)


def attn_shortconv_mixer(h, w_in, q_g, k_g, conv_w, w_out, tables):
    B, S = h.shape[0], h.shape[1]
    proj = h @ w_in
    splits = np.cumsum([ATTN_DIM, KV_DIM, KV_DIM, CONV_DIM, CONV_DIM]).tolist()
    q, k, v, gate_b, gate_c, conv_in = jnp.split(proj, splits, axis=-1)
    q = apply_axial_rope(rmsnorm(q.reshape(B, S, N_Q_HEADS, HEAD_DIM), q_g), tables)
    k = apply_axial_rope(rmsnorm(k.reshape(B, S, N_KV_HEADS, HEAD_DIM), k_g), tables)
    v = v.reshape(B, S, N_KV_HEADS, HEAD_DIM)
    attn_out = gqa_block_attention(q, k, v)
    conv_out = gate_b * dwconv3(gate_c * conv_in, conv_w)
    return jnp.concatenate([attn_out, conv_out], axis=-1) @ w_out


def pool_mixer(h, pool_w, pool_scale):
    S = h.shape[1]
    t = jnp.arange(S)
    outs = []
    for gi, w in enumerate(POOL_WINDOWS):
        xg = h[..., gi * POOL_GROUP:(gi + 1) * POOL_GROUP]
        xf = xg.astype(jnp.float32)
        cs = jnp.pad(jnp.cumsum(xf, axis=1), ((0, 0), (1, 0), (0, 0)))
        lo = jnp.clip(t - w // 2, 0, S)
        hi = jnp.clip(t + w // 2, 0, S)
        cnt = (hi - lo).astype(jnp.float32)[None, :, None]
        mean = (cs[:, hi] - cs[:, lo]) / cnt
        pooled = (mean - xf).astype(h.dtype)
        outs.append(jnp.einsum('bsc,cd->bsd', pooled, pool_w[gi]))
    return jnp.concatenate(outs, axis=-1) * pool_scale


def conv_ffn(h, w_up, conv_w, w_down):
    up = dwconv3(h @ w_up, conv_w)
    u, gt = jnp.split(up, 2, axis=-1)
    return (jax.nn.gelu(gt, approximate=True) * u) @ w_down


def setup_inputs(seed: int = 0) -> dict:
    key = jax.random.key(seed)
    ks = jax.random.split(key, 12)
    f32 = jnp.float32
    x = jax.random.normal(ks[0], (BATCH, SEQ, D_MODEL), f32)
    norm_g = 1.0 + 0.05 * jax.random.normal(ks[1], (DEPTH, 4, D_MODEL), f32)
    mix_w_in = jax.random.normal(ks[2], (N_EVEN, D_MODEL, IN_DIM), f32) * D_MODEL ** -0.5
    q_norm_g = 1.0 + 0.05 * jax.random.normal(ks[3], (N_EVEN, HEAD_DIM), f32)
    k_norm_g = 1.0 + 0.05 * jax.random.normal(ks[4], (N_EVEN, HEAD_DIM), f32)
    sconv_w = jax.random.normal(ks[5], (N_EVEN, 3, CONV_DIM), f32) * 3 ** -0.5
    mix_w_out = jax.random.normal(ks[6], (N_EVEN, MIX_DIM, D_MODEL), f32) * MIX_DIM ** -0.5
    pool_w = jax.random.normal(ks[7], (N_ODD, len(POOL_WINDOWS), POOL_GROUP, POOL_GROUP), f32) * POOL_GROUP ** -0.5
    pool_scale = 1.0 + 0.1 * jax.random.normal(ks[8], (N_ODD, D_MODEL), f32)
    ffn_w_up = jax.random.normal(ks[9], (DEPTH, D_MODEL, 2 * D_FF), f32) * D_MODEL ** -0.5
    ffn_conv_w = jax.random.normal(ks[10], (DEPTH, 3, 2 * D_FF), f32) * 3 ** -0.5
    ffn_w_down = jax.random.normal(ks[11], (DEPTH, D_FF, D_MODEL), f32) * D_FF ** -0.5
    return {"x": x, "norm_g": norm_g, "mix_w_in": mix_w_in, "q_norm_g": q_norm_g,
            "k_norm_g": k_norm_g, "sconv_w": sconv_w, "mix_w_out": mix_w_out,
            "pool_w": pool_w, "pool_scale": pool_scale, "ffn_w_up": ffn_w_up,
            "ffn_conv_w": ffn_conv_w, "ffn_w_down": ffn_w_down}


def reference(x, norm_g, mix_w_in, q_norm_g, k_norm_g, sconv_w, mix_w_out,
              pool_w, pool_scale, ffn_w_up, ffn_conv_w, ffn_w_down):
    rows = x.shape[1] // GRID_W
    tables = axial_rope_tables(rows)
    for i in range(DEPTH):
        g = norm_g[i]
        h = rmsnorm(x, g[0])
        if i % 2 == 0:
            e = i // 2
            mix = attn_shortconv_mixer(h, mix_w_in[e], q_norm_g[e], k_norm_g[e],
                                       sconv_w[e], mix_w_out[e], tables)
        else:
            o = i // 2
            mix = pool_mixer(h, pool_w[o], pool_scale[o])
        x = x + rmsnorm(mix, g[1])
        y = conv_ffn(rmsnorm(x, g[2]), ffn_w_up[i], ffn_conv_w[i], ffn_w_down[i])
        x = x + rmsnorm(y, g[3])
    return x
```

```python
import functools

import jax
import jax.numpy as jnp
from jax import lax
from jax.experimental import pallas as pl
from jax.experimental.pallas import tpu as pltpu

F32 = jnp.float32
BF16 = jnp.bfloat16

GRID_W = 64
HEAD_DIM = 64
N_Q_HEADS = 8
N_KV_HEADS = 2
ATTN_DIM = N_Q_HEADS * HEAD_DIM
KV_DIM = N_KV_HEADS * HEAD_DIM
CONV_DIM = 512
ROPE_THETA = 10000.0
ROPE_AXIS_DIM = HEAD_DIM // 2
POOL_WINDOWS = (2, 4, 8, 16)
RMS_EPS = 1e-6

LANES = 128
SUBLANES = 8
HALO = SUBLANES
MXU_DIM = 256

VMEM_LIMIT = 56 * 1024 * 1024

TM_PROJ = 512
TM_FFN = 512
TM_POOL = 256
TQ_ATTN = 256
FFN_CHUNK = 256


def _rmsnorm(x, g):
    ms = jnp.mean(x * x, axis=-1, keepdims=True)
    return x * lax.rsqrt(ms + RMS_EPS) * g


def _dot(a, b):
    return jnp.dot(a, b, preferred_element_type=F32)


def _const_spec(shape):
    zeros = (0,) * len(shape)
    return pl.BlockSpec(shape, lambda *_: zeros, pipeline_mode=pl.Buffered(1))


def _halo_specs(tm, n_rows, d):
    per = tm // HALO
    last = n_rows // HALO - 1
    main = pl.BlockSpec((tm, d), lambda i: (i, 0))
    prev = pl.BlockSpec((HALO, d), lambda i: (jnp.maximum(i * per - 1, 0), 0))
    nxt = pl.BlockSpec((HALO, d), lambda i: (jnp.minimum((i + 1) * per, last), 0))
    return main, prev, nxt


def _load_ext(x_ref, xp_ref, xn_ref, tiles_per_seq):
    pos = pl.program_id(0) % tiles_per_seq
    xp = jnp.where(pos != 0, xp_ref[...], 0.0)
    xn = jnp.where(pos != tiles_per_seq - 1, xn_ref[...], 0.0)
    return jnp.concatenate([x_ref[...], xp, xn], axis=0)


def _shift_rows(pm, prev_row, next_row, rows):
    tm = pm.shape[0]
    m1 = jnp.where(rows == 0, prev_row, pltpu.roll(pm, 1, 0))
    p1 = jnp.where(rows == tm - 1, next_row, pltpu.roll(pm, tm - 1, 0))
    return m1, p1


def _dwconv3_ext(p, cw, rows):
    tm = p.shape[0] - 2 * HALO
    pm = p[:tm]
    m1, p1 = _shift_rows(pm, p[tm + HALO - 1:tm + HALO], p[tm + HALO:tm + HALO + 1], rows)
    return m1 * cw[0:1] + pm * cw[1:2] + p1 * cw[2:3]


def _head_mean_square(v, bd):
    sq = v * v
    hi = sq.astype(BF16)
    lo = (sq - hi.astype(F32)).astype(BF16)
    outs = []
    for c0 in range(0, v.shape[1], MXU_DIM):
        wd = min(MXU_DIM, v.shape[1] - c0)
        m = bd[:wd, :wd]
        outs.append(_dot(hi[:, c0:c0 + wd], m) + _dot(lo[:, c0:c0 + wd], m))
    return outs[0] if len(outs) == 1 else jnp.concatenate(outs, axis=1)


def _rope(v, c, su, sd):
    outs = []
    for s in range(v.shape[1] // LANES):
        vs = v[:, s * LANES:(s + 1) * LANES]
        up = pltpu.roll(vs, ROPE_AXIS_DIM // 2, 1)
        dn = pltpu.roll(vs, LANES - ROPE_AXIS_DIM // 2, 1)
        outs.append(vs * c + up * su + dn * sd)
    return outs[0] if len(outs) == 1 else jnp.concatenate(outs, axis=1)


def _inproj_kernel(x_ref, xp_ref, xn_ref, g_ref, w_ref, qg_ref, kg_ref, bd_ref,
                   c_ref, su_ref, sd_ref, cw_ref,
                   q_out, ka_out, kb_out, va_out, vb_out, conv_out,
                   *, tm, tiles_per_seq):
    x_ext = _load_ext(x_ref, xp_ref, xn_ref, tiles_per_seq)
    h_ext = _rmsnorm(x_ext, g_ref[...]).astype(BF16)
    n_head_cols = ATTN_DIM + 2 * KV_DIM + CONV_DIM
    main = _dot(h_ext[:tm], w_ref[:, :n_head_cols])
    gc = _dot(h_ext, w_ref[:, n_head_cols:])

    bd = bd_ref[...]
    c, su, sd = c_ref[...], su_ref[...], sd_ref[...]

    q_raw = main[:, :ATTN_DIM]
    qn = q_raw * lax.rsqrt(_head_mean_square(q_raw, bd) + RMS_EPS) * qg_ref[...]
    q_out[...] = (_rope(qn, c, su, sd) * (HEAD_DIM ** -0.5)).astype(q_out.dtype)

    k_raw = main[:, ATTN_DIM:ATTN_DIM + KV_DIM]
    kn = k_raw * lax.rsqrt(_head_mean_square(k_raw, bd) + RMS_EPS) * kg_ref[...]
    kr = _rope(kn, c, su, sd)
    v = main[:, ATTN_DIM + KV_DIM:ATTN_DIM + 2 * KV_DIM]

    low = lax.broadcasted_iota(jnp.int32, (tm, LANES), 1) < HEAD_DIM
    for arr, a_out, b_out in ((kr, ka_out, kb_out), (v, va_out, vb_out)):
        sw = pltpu.roll(arr, HEAD_DIM, 1)
        a_out[0] = jnp.where(low, arr, 0.0).astype(a_out.dtype)
        b_out[0] = jnp.where(low, 0.0, sw).astype(b_out.dtype)
        a_out[1] = jnp.where(low, sw, 0.0).astype(a_out.dtype)
        b_out[1] = jnp.where(low, 0.0, arr).astype(b_out.dtype)

    gate_b = main[:, ATTN_DIM + 2 * KV_DIM:]
    z = gc[:, :CONV_DIM] * gc[:, CONV_DIM:]
    rows = lax.broadcasted_iota(jnp.int32, (tm, CONV_DIM), 0)
    conv_out[...] = (gate_b * _dwconv3_ext(z, cw_ref[...], rows)).astype(conv_out.dtype)


def _inproj(x2d, g, w_in, qg, kg, bd, c_tab, su_tab, sd_tab, conv_w, *, seq):
    n_rows, d = x2d.shape
    tm = TM_PROJ
    tiles_per_seq = seq // tm
    main, prev, nxt = _halo_specs(tm, n_rows, d)
    tab = pl.BlockSpec((tm, LANES), lambda i: (i % tiles_per_seq, 0))
    kv_spec = pl.BlockSpec((N_KV_HEADS, tm, LANES), lambda i: (0, i, 0))
    kv_shape = jax.ShapeDtypeStruct((N_KV_HEADS, n_rows, LANES), BF16)
    return pl.pallas_call(
        functools.partial(_inproj_kernel, tm=tm, tiles_per_seq=tiles_per_seq),
        grid=(n_rows // tm,),
        in_specs=[main, prev, nxt, _const_spec(g.shape), _const_spec(w_in.shape),
                  _const_spec(qg.shape), _const_spec(kg.shape), _const_spec(bd.shape),
                  tab, tab, tab, _const_spec(conv_w.shape)],
        out_specs=[pl.BlockSpec((tm, ATTN_DIM), lambda i: (i, 0)),
                   kv_spec, kv_spec, kv_spec, kv_spec,
                   pl.BlockSpec((tm, CONV_DIM), lambda i: (i, 0))],
        out_shape=[jax.ShapeDtypeStruct((n_rows, ATTN_DIM), BF16),
                   kv_shape, kv_shape, kv_shape, kv_shape,
                   jax.ShapeDtypeStruct((n_rows, CONV_DIM), BF16)],
        compiler_params=pltpu.CompilerParams(
            dimension_semantics=("parallel",), vmem_limit_bytes=VMEM_LIMIT),
        name="inproj",
    )(x2d, x2d, x2d, g, w_in, qg, kg, bd, c_tab, su_tab, sd_tab, conv_w)


def _attn_kernel(q_ref, ka_ref, kb_ref, va_ref, vb_ref, o_ref):
    nt = (((1,), (1,)), ((), ()))
    for j in range(q_ref.shape[1] // LANES):
        qj = q_ref[:, j * LANES:(j + 1) * LANES]
        out = None
        for k_ref, v_ref in ((ka_ref, va_ref), (kb_ref, vb_ref)):
            s = lax.dot_general(qj, k_ref[...], nt, preferred_element_type=F32)
            p = jnp.exp(s - jnp.max(s, axis=-1, keepdims=True))
            l = jnp.sum(p, axis=-1, keepdims=True)
            o = _dot(p.astype(BF16), v_ref[...]) * (1.0 / l)
            out = o if out is None else out + o
        o_ref[:, j * LANES:(j + 1) * LANES] = out.astype(o_ref.dtype)


def _attention(q, ka, kb, va, vb, *, batch, seq):
    tq = TQ_ATTN
    q_tiles = seq // tq
    group_cols = ATTN_DIM // N_KV_HEADS
    q_spec = pl.BlockSpec((tq, group_cols), lambda b, h, i: (b * q_tiles + i, h))
    kv_spec = pl.BlockSpec((None, seq, LANES), lambda b, h, i: (h, b, 0))
    return pl.pallas_call(
        _attn_kernel,
        grid=(batch, N_KV_HEADS, q_tiles),
        in_specs=[q_spec, kv_spec, kv_spec, kv_spec, kv_spec],
        out_specs=q_spec,
        out_shape=jax.ShapeDtypeStruct(q.shape, BF16),
        compiler_params=pltpu.CompilerParams(
            dimension_semantics=("parallel", "parallel", "arbitrary"),
            vmem_limit_bytes=VMEM_LIMIT),
        name="attention",
    )(q, ka, kb, va, vb)


def _outproj_kernel(a_ref, c_ref, wa_ref, wc_ref, x_ref, g_ref, o_ref):
    mix = _dot(a_ref[...], wa_ref[...]) + _dot(c_ref[...], wc_ref[...])
    o_ref[...] = x_ref[...] + _rmsnorm(mix, g_ref[...])


def _outproj(attn, conv, w_attn, w_conv, x2d, g):
    n_rows, d = x2d.shape
    tm = TM_PROJ
    row = lambda cols: pl.BlockSpec((tm, cols), lambda i: (i, 0))
    return pl.pallas_call(
        _outproj_kernel,
        grid=(n_rows // tm,),
        in_specs=[row(ATTN_DIM), row(CONV_DIM), _const_spec(w_attn.shape),
                  _const_spec(w_conv.shape), row(d), _const_spec(g.shape)],
        out_specs=row(d),
        out_shape=jax.ShapeDtypeStruct(x2d.shape, x2d.dtype),
        compiler_params=pltpu.CompilerParams(
            dimension_semantics=("parallel",), vmem_limit_bytes=VMEM_LIMIT),
        name="outproj",
    )(attn, conv, w_attn, w_conv, x2d, g)


def _pool_kernel(x_ref, xp_ref, xn_ref, g0_ref, pw_ref, ps_ref, g1_ref, o_ref,
                 *, tm, tiles_per_seq, seq):
    pos = pl.program_id(0) % tiles_per_seq
    xp = jnp.where(pos != 0, xp_ref[...], 0.0)
    xn = jnp.where(pos != tiles_per_seq - 1, xn_ref[...], 0.0)
    x = x_ref[...]
    g0 = g0_ref[...]
    h_ext = _rmsnorm(jnp.concatenate([xp, x, xn], axis=0), g0)
    n_ext = tm + 2 * HALO
    group = x.shape[1] // len(POOL_WINDOWS)
    t = pos * tm + lax.broadcasted_iota(jnp.int32, (tm, group), 0)
    outs = []
    for gi, w in enumerate(POOL_WINDOWS):
        hg = h_ext[:, gi * group:(gi + 1) * group]
        acc, span = hg, 1
        while span < w:
            acc = acc + pltpu.roll(acc, span, 0)
            span *= 2
        ahead = w // 2 - 1
        if ahead:
            acc = pltpu.roll(acc, n_ext - ahead, 0)
        win = acc[HALO:HALO + tm]
        cnt = (jnp.minimum(t + w // 2, seq) - jnp.maximum(t - w // 2, 0)).astype(F32)
        pooled = win / cnt - hg[HALO:HALO + tm]
        outs.append(_dot(pooled.astype(BF16), pw_ref[gi]))
    mix = jnp.concatenate(outs, axis=1) * ps_ref[...]
    o_ref[...] = x + _rmsnorm(mix, g1_ref[...])


def _pool(x2d, g0, pool_w, pool_scale, g1, *, seq):
    n_rows, d = x2d.shape
    tm = TM_POOL
    tiles_per_seq = seq // tm
    main, prev, nxt = _halo_specs(tm, n_rows, d)
    return pl.pallas_call(
        functools.partial(_pool_kernel, tm=tm, tiles_per_seq=tiles_per_seq, seq=seq),
        grid=(n_rows // tm,),
        in_specs=[main, prev, nxt, _const_spec(g0.shape), _const_spec(pool_w.shape),
                  _const_spec(pool_scale.shape), _const_spec(g1.shape)],
        out_specs=pl.BlockSpec((tm, d), lambda i: (i, 0)),
        out_shape=jax.ShapeDtypeStruct(x2d.shape, x2d.dtype),
        compiler_params=pltpu.CompilerParams(
            dimension_semantics=("parallel",), vmem_limit_bytes=VMEM_LIMIT),
        name="pool",
    )(x2d, x2d, x2d, g0, pool_w, pool_scale, g1)


def _ffn_kernel(x_ref, xp_ref, xn_ref, g2_ref, wup_ref, cw_ref, wdn_ref, g3_ref,
                o_ref, act_ref, *, tm, tiles_per_seq, chunk):
    x_ext = _load_ext(x_ref, xp_ref, xn_ref, tiles_per_seq)
    h_ext = _rmsnorm(x_ext, g2_ref[...]).astype(BF16)
    rows = lax.broadcasted_iota(jnp.int32, (tm, 2 * chunk), 0)
    for ci in range(act_ref.shape[1] // chunk):
        c0 = 2 * chunk * ci
        p = _dot(h_ext, wup_ref[:, c0:c0 + 2 * chunk])
        up = _dwconv3_ext(p, cw_ref[:, c0:c0 + 2 * chunk], rows)
        u, gt = up[:, :chunk], up[:, chunk:]
        act = jax.nn.gelu(gt, approximate=True) * u
        act_ref[:, ci * chunk:(ci + 1) * chunk] = act.astype(act_ref.dtype)
    y = _dot(act_ref[...], wdn_ref[...])
    o_ref[...] = x_ref[...] + _rmsnorm(y, g3_ref[...])


def _ffn(x2d, g2, w_up, conv_w, w_down, g3, *, seq):
    n_rows, d = x2d.shape
    d_ff = w_down.shape[0]
    tm = TM_FFN
    tiles_per_seq = seq // tm
    main, prev, nxt = _halo_specs(tm, n_rows, d)
    return pl.pallas_call(
        functools.partial(_ffn_kernel, tm=tm, tiles_per_seq=tiles_per_seq, chunk=FFN_CHUNK),
        grid=(n_rows // tm,),
        in_specs=[main, prev, nxt, _const_spec(g2.shape), _const_spec(w_up.shape),
                  _const_spec(conv_w.shape), _const_spec(w_down.shape), _const_spec(g3.shape)],
        out_specs=pl.BlockSpec((tm, d), lambda i: (i, 0)),
        out_shape=jax.ShapeDtypeStruct(x2d.shape, x2d.dtype),
        scratch_shapes=[pltpu.VMEM((tm, d_ff), BF16)],
        compiler_params=pltpu.CompilerParams(
            dimension_semantics=("parallel",), vmem_limit_bytes=VMEM_LIMIT),
        name="ffn",
    )(x2d, x2d, x2d, g2, w_up, conv_w, w_down, g3)


def _rope_tables(seq):
    rows = seq // GRID_W
    row = jnp.repeat(jnp.arange(rows, dtype=F32), GRID_W)
    col = jnp.tile(jnp.arange(GRID_W, dtype=F32), rows)
    inv = ROPE_THETA ** (-jnp.arange(0, ROPE_AXIS_DIM, 2, dtype=F32) / ROPE_AXIS_DIM)
    ang_r = row[:, None] * inv[None, :]
    ang_c = col[:, None] * inv[None, :]
    cos_r, sin_r, cos_c, sin_c = jnp.cos(ang_r), jnp.sin(ang_r), jnp.cos(ang_c), jnp.sin(ang_c)
    zero = jnp.zeros_like(sin_r)
    c = jnp.concatenate([cos_r, cos_r, cos_c, cos_c], axis=1)
    su = jnp.concatenate([zero, sin_r, zero, sin_c], axis=1)
    sd = jnp.concatenate([-sin_r, zero, -sin_c, zero], axis=1)
    reps = LANES // HEAD_DIM
    return tuple(jnp.tile(t, (1, reps)) for t in (c, su, sd))


def _interleave_ffn_cols(a, d_ff, chunk):
    lead = a.shape[:-1]
    u = a[..., :d_ff].reshape(*lead, d_ff // chunk, 1, chunk)
    g = a[..., d_ff:].reshape(*lead, d_ff // chunk, 1, chunk)
    return jnp.concatenate([u, g], axis=-2).reshape(*lead, 2 * d_ff)


def kernel(x, norm_g, mix_w_in, q_norm_g, k_norm_g, sconv_w, mix_w_out, pool_w, pool_scale,
           ffn_w_up, ffn_conv_w, ffn_w_down):
    batch, seq, d = x.shape
    depth = norm_g.shape[0]
    d_ff = ffn_w_down.shape[1]
    assert seq % TM_PROJ == 0 and seq % TM_FFN == 0 and seq % TM_POOL == 0 and seq % TQ_ATTN == 0
    assert d_ff % FFN_CHUNK == 0

    c_tab, su_tab, sd_tab = _rope_tables(seq)
    lane_head = jnp.arange(MXU_DIM) // HEAD_DIM
    bd = jnp.where(lane_head[:, None] == lane_head[None, :], 1.0 / HEAD_DIM, 0.0).astype(BF16)

    x2d = x.reshape(batch * seq, d)
    for i in range(depth):
        g = norm_g[i]
        if i % 2 == 0:
            e = i // 2
            qg = jnp.tile(q_norm_g[e], N_Q_HEADS)[None, :]
            kg = jnp.tile(k_norm_g[e], N_KV_HEADS)[None, :]
            q, ka, kb, va, vb, conv = _inproj(
                x2d, g[0:1], mix_w_in[e].astype(BF16), qg, kg, bd, c_tab, su_tab, sd_tab,
                sconv_w[e], seq=seq)
            attn = _attention(q, ka, kb, va, vb, batch=batch, seq=seq)
            w_out = mix_w_out[e].astype(BF16)
            x2d = _outproj(attn, conv, w_out[:ATTN_DIM], w_out[ATTN_DIM:], x2d, g[1:2])
        else:
            o = i // 2
            x2d = _pool(x2d, g[0:1], pool_w[o].astype(BF16), pool_scale[o][None, :], g[1:2], seq=seq)
        w_up = _interleave_ffn_cols(ffn_w_up[i], d_ff, FFN_CHUNK).astype(BF16)
        conv_w = _interleave_ffn_cols(ffn_conv_w[i], d_ff, FFN_CHUNK)
        x2d = _ffn(x2d, g[2:3], w_up, conv_w, ffn_w_down[i].astype(BF16), g[3:4], seq=seq)
    return x2d.reshape(batch, seq, d)
```

```python
import functools
import math

import jax
import jax.numpy as jnp
from jax import lax
from jax.experimental import pallas as pl
from jax.experimental.pallas import tpu as pltpu

F32 = jnp.float32
BF16 = jnp.bfloat16

GRID_W = 64
HEAD_DIM = 64
N_Q_HEADS = 8
N_KV_HEADS = 2
Q_GROUP = N_Q_HEADS // N_KV_HEADS
ATTN_DIM = N_Q_HEADS * HEAD_DIM
KV_DIM = N_KV_HEADS * HEAD_DIM
CONV_DIM = 512
ROPE_THETA = 10000.0
ROPE_AXIS_DIM = HEAD_DIM // 2
POOL_WINDOWS = (2, 4, 8, 16)
RMS_EPS = 1e-6

LANES = 128
SUBLANES = 8
HALO = SUBLANES
MXU_DIM = 256

VMEM_LIMIT = 56 * 1024 * 1024

TM_PROJ = 512
TM_FFN = 512
TM_POOL = 256
TQ_ATTN = 512
FFN_CHUNK = 256


def _rmsnorm(x, g):
    ms = jnp.mean(x * x, axis=-1, keepdims=True)
    return x * lax.rsqrt(ms + RMS_EPS) * g


def _dot(a, b):
    return jnp.dot(a, b, preferred_element_type=F32)


def _const_spec(shape):
    zeros = (0,) * len(shape)
    return pl.BlockSpec(shape, lambda *_: zeros, pipeline_mode=pl.Buffered(1))


def _layer_spec(arr, layer, block=None, pos=None):
    block = tuple(arr.shape[1:]) if block is None else tuple(block)
    pos = (0,) * len(block) if pos is None else tuple(pos)
    index = (layer,) + pos
    return pl.BlockSpec((None,) + block, lambda *_: index, pipeline_mode=pl.Buffered(1))


def _halo_specs(tm, n_rows, d):
    per = tm // HALO
    last = n_rows // HALO - 1
    main = pl.BlockSpec((tm, d), lambda i: (i, 0))
    prev = pl.BlockSpec((HALO, d), lambda i: (jnp.maximum(i * per - 1, 0), 0))
    nxt = pl.BlockSpec((HALO, d), lambda i: (jnp.minimum((i + 1) * per, last), 0))
    return main, prev, nxt


def _load_ext(x_ref, xp_ref, xn_ref, tiles_per_seq):
    pos = pl.program_id(0) % tiles_per_seq
    xp = jnp.where(pos != 0, xp_ref[...], 0.0)
    xn = jnp.where(pos != tiles_per_seq - 1, xn_ref[...], 0.0)
    return jnp.concatenate([x_ref[...], xp, xn], axis=0)


def _shift_rows(pm, prev_row, next_row):
    tm = pm.shape[0]
    r = lax.broadcasted_iota(jnp.int32, (SUBLANES, pm.shape[1]), 0)
    m1 = pltpu.roll(pm, 1, 0)
    p1 = pltpu.roll(pm, tm - 1, 0)
    m1 = jnp.concatenate(
        [jnp.where(r == 0, prev_row, m1[:SUBLANES]), m1[SUBLANES:]], axis=0)
    p1 = jnp.concatenate(
        [p1[:tm - SUBLANES], jnp.where(r == SUBLANES - 1, next_row, p1[tm - SUBLANES:])], axis=0)
    return m1, p1


def _dwconv3_ext(p, cw):
    tm = p.shape[0] - 2 * HALO
    pm = p[:tm]
    m1, p1 = _shift_rows(pm, p[tm + HALO - 1:tm + HALO], p[tm + HALO:tm + HALO + 1])
    return m1 * cw[0:1] + pm * cw[1:2] + p1 * cw[2:3]


def _head_mean_square(v, bd):
    sq = v * v
    hi = sq.astype(BF16)
    lo = (sq - hi.astype(F32)).astype(BF16)
    outs = []
    for c0 in range(0, v.shape[1], MXU_DIM):
        wd = min(MXU_DIM, v.shape[1] - c0)
        m = bd[:wd, :wd]
        outs.append(_dot(hi[:, c0:c0 + wd], m) + _dot(lo[:, c0:c0 + wd], m))
    return outs[0] if len(outs) == 1 else jnp.concatenate(outs, axis=1)


def _rope(v, c, su, sd):
    outs = []
    for s in range(v.shape[1] // LANES):
        vs = v[:, s * LANES:(s + 1) * LANES]
        up = pltpu.roll(vs, ROPE_AXIS_DIM // 2, 1)
        dn = pltpu.roll(vs, LANES - ROPE_AXIS_DIM // 2, 1)
        outs.append(vs * c + up * su + dn * sd)
    return outs[0] if len(outs) == 1 else jnp.concatenate(outs, axis=1)


def _inproj_kernel(x_ref, xp_ref, xn_ref, g_ref, w_ref, qg_ref, kg_ref, bd_ref,
                   c_ref, su_ref, sd_ref, cw_ref,
                   qt_out, ka_out, kb_out, vt_out, conv_out,
                   *, tm, tiles_per_seq, q_scale):
    x_ext = _load_ext(x_ref, xp_ref, xn_ref, tiles_per_seq)
    h_ext = _rmsnorm(x_ext, g_ref[...]).astype(BF16)
    n_head_cols = ATTN_DIM + 2 * KV_DIM + CONV_DIM
    main = _dot(h_ext[:tm], w_ref[:, :n_head_cols])
    gc = _dot(h_ext, w_ref[:, n_head_cols:])

    bd = bd_ref[...]
    c, su, sd = c_ref[...], su_ref[...], sd_ref[...]

    q_raw = main[:, :ATTN_DIM]
    qn = q_raw * lax.rsqrt(_head_mean_square(q_raw, bd) + RMS_EPS) * qg_ref[...]
    qt_out[...] = (_rope(qn, c, su, sd) * q_scale).T.astype(qt_out.dtype)
    vt_out[...] = main[:, ATTN_DIM + KV_DIM:ATTN_DIM + 2 * KV_DIM].T.astype(vt_out.dtype)

    k_raw = main[:, ATTN_DIM:ATTN_DIM + KV_DIM]
    kn = k_raw * lax.rsqrt(_head_mean_square(k_raw, bd) + RMS_EPS) * kg_ref[...]
    kr = _rope(kn, c, su, sd)
    sw = pltpu.roll(kr, HEAD_DIM, 1)
    low = lax.broadcasted_iota(jnp.int32, (tm, LANES), 1) < HEAD_DIM
    ka_out[0] = jnp.where(low, kr, 0.0).astype(ka_out.dtype)
    kb_out[0] = jnp.where(low, 0.0, sw).astype(kb_out.dtype)
    ka_out[1] = jnp.where(low, sw, 0.0).astype(ka_out.dtype)
    kb_out[1] = jnp.where(low, 0.0, kr).astype(kb_out.dtype)

    gate_b = main[:, ATTN_DIM + 2 * KV_DIM:]
    z = gc[:, :CONV_DIM] * gc[:, CONV_DIM:]
    conv_out[...] = (gate_b * _dwconv3_ext(z, cw_ref[...])).astype(conv_out.dtype)


def _inproj(x2d, g_all, g_row, w_in, layer, qg, kg, bd, c_tab, su_tab, sd_tab, sconv_w, *, seq):
    n_rows, d = x2d.shape
    tm = TM_PROJ
    tiles_per_seq = seq // tm
    main, prev, nxt = _halo_specs(tm, n_rows, d)
    tab = pl.BlockSpec((tm, LANES), lambda i: (i % tiles_per_seq, 0))
    k_spec = pl.BlockSpec((N_KV_HEADS, tm, LANES), lambda i: (0, i, 0))
    k_shape = jax.ShapeDtypeStruct((N_KV_HEADS, n_rows, LANES), BF16)
    q_scale = HEAD_DIM ** -0.5 * math.log2(math.e)
    return pl.pallas_call(
        functools.partial(_inproj_kernel, tm=tm, tiles_per_seq=tiles_per_seq, q_scale=q_scale),
        grid=(n_rows // tm,),
        in_specs=[main, prev, nxt, _layer_spec(g_all, g_row), _layer_spec(w_in, layer),
                  _const_spec(qg.shape), _const_spec(kg.shape), _const_spec(bd.shape),
                  tab, tab, tab, _layer_spec(sconv_w, layer)],
        out_specs=[pl.BlockSpec((ATTN_DIM, tm), lambda i: (0, i)),
                   k_spec, k_spec,
                   pl.BlockSpec((KV_DIM, tm), lambda i: (0, i)),
                   pl.BlockSpec((tm, CONV_DIM), lambda i: (i, 0))],
        out_shape=[jax.ShapeDtypeStruct((ATTN_DIM, n_rows), BF16),
                   k_shape, k_shape,
                   jax.ShapeDtypeStruct((KV_DIM, n_rows), BF16),
                   jax.ShapeDtypeStruct((n_rows, CONV_DIM), BF16)],
        compiler_params=pltpu.CompilerParams(
            dimension_semantics=("parallel",), vmem_limit_bytes=VMEM_LIMIT),
        name="inproj",
    )(x2d, x2d, x2d, g_all, w_in, qg, kg, bd, c_tab, su_tab, sd_tab, sconv_w)


def _attn_kernel(qt_ref, ka_ref, kb_ref, vt_ref, o_ref):
    vt = vt_ref[...]
    sts = []
    for g, k_ref in enumerate((ka_ref, kb_ref, ka_ref, kb_ref)):
        slab = g // 2
        sts.append(_dot(k_ref[...], qt_ref[slab * LANES:(slab + 1) * LANES, :]))
    outs = []
    for pair in range(0, len(sts), 2):
        ps, invs = [], []
        for st in sts[pair:pair + 2]:
            p = jnp.exp2(st - jnp.max(st, axis=0, keepdims=True))
            invs.append(1.0 / jnp.sum(p, axis=0, keepdims=True))
            ps.append(p.astype(BF16))
        for p, inv in zip(ps, invs):
            outs.append(_dot(vt, p) * inv)
    o_ref[...] = jnp.concatenate(outs, axis=0).T.astype(o_ref.dtype)


def _attention(qt, ka, kb, vt, *, batch, seq):
    tq = TQ_ATTN
    q_tiles = seq // tq
    group_cols = Q_GROUP * HEAD_DIM
    k_spec = pl.BlockSpec((None, seq, LANES), lambda b, h, i: (h, b, 0))
    return pl.pallas_call(
        _attn_kernel,
        grid=(batch, N_KV_HEADS, q_tiles),
        in_specs=[pl.BlockSpec((group_cols, tq), lambda b, h, i: (h, b * q_tiles + i)),
                  k_spec, k_spec,
                  pl.BlockSpec((HEAD_DIM, seq), lambda b, h, i: (h, b))],
        out_specs=pl.BlockSpec((tq, group_cols), lambda b, h, i: (b * q_tiles + i, h)),
        out_shape=jax.ShapeDtypeStruct((qt.shape[1], qt.shape[0]), BF16),
        compiler_params=pltpu.CompilerParams(
            dimension_semantics=("parallel", "parallel", "arbitrary"),
            vmem_limit_bytes=VMEM_LIMIT),
        name="attention",
    )(qt, ka, kb, vt)


def _outproj_kernel(a_ref, c_ref, wa_ref, wc_ref, x_ref, g_ref, o_ref):
    mix = _dot(a_ref[...], wa_ref[...]) + _dot(c_ref[...], wc_ref[...])
    o_ref[...] = x_ref[...] + _rmsnorm(mix, g_ref[...])


def _outproj(attn, conv, w_out, layer, x2d, g_all, g_row):
    n_rows, d = x2d.shape
    tm = TM_PROJ
    row = lambda cols: pl.BlockSpec((tm, cols), lambda i: (i, 0))
    return pl.pallas_call(
        _outproj_kernel,
        grid=(n_rows // tm,),
        in_specs=[row(ATTN_DIM), row(CONV_DIM),
                  _layer_spec(w_out, layer, (ATTN_DIM, d), (0, 0)),
                  _layer_spec(w_out, layer, (CONV_DIM, d), (ATTN_DIM // CONV_DIM, 0)),
                  row(d), _layer_spec(g_all, g_row)],
        out_specs=row(d),
        out_shape=jax.ShapeDtypeStruct(x2d.shape, x2d.dtype),
        compiler_params=pltpu.CompilerParams(
            dimension_semantics=("parallel",), vmem_limit_bytes=VMEM_LIMIT),
        name="outproj",
    )(attn, conv, w_out, w_out, x2d, g_all)


def _pool_kernel(x_ref, xp_ref, xn_ref, g0_ref, pw_ref, ps_ref, g1_ref, o_ref,
                 *, tm, tiles_per_seq, seq):
    pos = pl.program_id(0) % tiles_per_seq
    xp = jnp.where(pos != 0, xp_ref[...], 0.0)
    xn = jnp.where(pos != tiles_per_seq - 1, xn_ref[...], 0.0)
    x = x_ref[...]
    h_ext = _rmsnorm(jnp.concatenate([xp, x, xn], axis=0), g0_ref[...])
    n_ext = tm + 2 * HALO
    group = x.shape[1] // len(POOL_WINDOWS)
    t = pos * tm + lax.broadcasted_iota(jnp.int32, (tm, group), 0)
    outs = []
    for gi, w in enumerate(POOL_WINDOWS):
        hg = h_ext[:, gi * group:(gi + 1) * group]
        acc, span = hg, 1
        while span < w:
            acc = acc + pltpu.roll(acc, span, 0)
            span *= 2
        ahead = w // 2 - 1
        if ahead:
            acc = pltpu.roll(acc, n_ext - ahead, 0)
        win = acc[HALO:HALO + tm]
        cnt = (jnp.minimum(t + w // 2, seq) - jnp.maximum(t - w // 2, 0)).astype(F32)
        pooled = win / cnt - hg[HALO:HALO + tm]
        outs.append(_dot(pooled.astype(BF16), pw_ref[gi]))
    mix = jnp.concatenate(outs, axis=1) * ps_ref[...]
    o_ref[...] = x + _rmsnorm(mix, g1_ref[...])


def _pool(x2d, g_all, g_row, pool_w, pool_scale, layer, *, seq):
    n_rows, d = x2d.shape
    tm = TM_POOL
    tiles_per_seq = seq // tm
    main, prev, nxt = _halo_specs(tm, n_rows, d)
    return pl.pallas_call(
        functools.partial(_pool_kernel, tm=tm, tiles_per_seq=tiles_per_seq, seq=seq),
        grid=(n_rows // tm,),
        in_specs=[main, prev, nxt, _layer_spec(g_all, g_row), _layer_spec(pool_w, layer),
                  _layer_spec(pool_scale, layer), _layer_spec(g_all, g_row + 1)],
        out_specs=pl.BlockSpec((tm, d), lambda i: (i, 0)),
        out_shape=jax.ShapeDtypeStruct(x2d.shape, x2d.dtype),
        compiler_params=pltpu.CompilerParams(
            dimension_semantics=("parallel",), vmem_limit_bytes=VMEM_LIMIT),
        name="pool",
    )(x2d, x2d, x2d, g_all, pool_w, pool_scale, g_all)


def _ffn_kernel(x_ref, xp_ref, xn_ref, g2_ref, wup_ref, cw_ref, wdn_ref, g3_ref,
                o_ref, act_ref, *, tiles_per_seq, chunk):
    d_ff = act_ref.shape[1]
    x_ext = _load_ext(x_ref, xp_ref, xn_ref, tiles_per_seq)
    h_ext = _rmsnorm(x_ext, g2_ref[...]).astype(BF16)

    def up_branch(c0):
        p = _dot(h_ext, wup_ref[:, c0:c0 + chunk])
        return _dwconv3_ext(p, cw_ref[:, c0:c0 + chunk])

    for c0 in range(0, d_ff, chunk):
        u = up_branch(c0)
        gt = up_branch(d_ff + c0)
        act = jax.nn.gelu(gt, approximate=True) * u
        act_ref[:, c0:c0 + chunk] = act.astype(act_ref.dtype)
    y = _dot(act_ref[...], wdn_ref[...])
    o_ref[...] = x_ref[...] + _rmsnorm(y, g3_ref[...])


def _ffn(x2d, g_all, g_row, w_up, conv_w, w_down, layer, *, seq):
    n_rows, d = x2d.shape
    d_ff = w_down.shape[1]
    tm = TM_FFN
    tiles_per_seq = seq // tm
    main, prev, nxt = _halo_specs(tm, n_rows, d)
    return pl.pallas_call(
        functools.partial(_ffn_kernel, tiles_per_seq=tiles_per_seq, chunk=FFN_CHUNK),
        grid=(n_rows // tm,),
        in_specs=[main, prev, nxt, _layer_spec(g_all, g_row), _layer_spec(w_up, layer),
                  _layer_spec(conv_w, layer), _layer_spec(w_down, layer),
                  _layer_spec(g_all, g_row + 1)],
        out_specs=pl.BlockSpec((tm, d), lambda i: (i, 0)),
        out_shape=jax.ShapeDtypeStruct(x2d.shape, x2d.dtype),
        scratch_shapes=[pltpu.VMEM((tm, d_ff), BF16)],
        compiler_params=pltpu.CompilerParams(
            dimension_semantics=("parallel",), vmem_limit_bytes=VMEM_LIMIT),
        name="ffn",
    )(x2d, x2d, x2d, g_all, w_up, conv_w, w_down, g_all)


def _rope_tables(seq):
    rows = seq // GRID_W
    row = jnp.repeat(jnp.arange(rows, dtype=F32), GRID_W)
    col = jnp.tile(jnp.arange(GRID_W, dtype=F32), rows)
    inv = ROPE_THETA ** (-jnp.arange(0, ROPE_AXIS_DIM, 2, dtype=F32) / ROPE_AXIS_DIM)
    ang_r = row[:, None] * inv[None, :]
    ang_c = col[:, None] * inv[None, :]
    cos_r, sin_r, cos_c, sin_c = jnp.cos(ang_r), jnp.sin(ang_r), jnp.cos(ang_c), jnp.sin(ang_c)
    zero = jnp.zeros_like(sin_r)
    c = jnp.concatenate([cos_r, cos_r, cos_c, cos_c], axis=1)
    su = jnp.concatenate([zero, sin_r, zero, sin_c], axis=1)
    sd = jnp.concatenate([-sin_r, zero, -sin_c, zero], axis=1)
    reps = LANES // HEAD_DIM
    return tuple(jnp.tile(t, (1, reps)) for t in (c, su, sd))


def kernel(x, norm_g, mix_w_in, q_norm_g, k_norm_g, sconv_w, mix_w_out, pool_w, pool_scale,
           ffn_w_up, ffn_conv_w, ffn_w_down):
    batch, seq, d = x.shape
    depth, norms_per_layer = norm_g.shape[:2]
    assert seq % TM_PROJ == 0 and seq % TM_FFN == 0 and seq % TM_POOL == 0 and seq % TQ_ATTN == 0
    assert ffn_w_down.shape[1] % FFN_CHUNK == 0

    c_tab, su_tab, sd_tab = _rope_tables(seq)
    lane_head = jnp.arange(MXU_DIM) // HEAD_DIM
    bd = jnp.where(lane_head[:, None] == lane_head[None, :], 1.0 / HEAD_DIM, 0.0).astype(BF16)

    w_in, w_out, pool_w_bf = mix_w_in.astype(BF16), mix_w_out.astype(BF16), pool_w.astype(BF16)
    w_up, w_down = ffn_w_up.astype(BF16), ffn_w_down.astype(BF16)
    g_all = norm_g.reshape(depth * norms_per_layer, 1, d)
    pool_scale3 = pool_scale[:, None, :]

    x2d = x.reshape(batch * seq, d)
    for i in range(depth):
        g_row = i * norms_per_layer
        if i % 2 == 0:
            e = i // 2
            qg = jnp.tile(q_norm_g[e], N_Q_HEADS)[None, :]
            kg = jnp.tile(k_norm_g[e], N_KV_HEADS)[None, :]
            qt, ka, kb, vt, conv = _inproj(
                x2d, g_all, g_row, w_in, e, qg, kg, bd, c_tab, su_tab, sd_tab, sconv_w, seq=seq)
            attn = _attention(qt, ka, kb, vt, batch=batch, seq=seq)
            x2d = _outproj(attn, conv, w_out, e, x2d, g_all, g_row + 1)
        else:
            x2d = _pool(x2d, g_all, g_row, pool_w_bf, pool_scale3, i // 2, seq=seq)
        x2d = _ffn(x2d, g_all, g_row + 2, w_up, ffn_conv_w, w_down, i, seq=seq)
    return x2d.reshape(batch, seq, d)
```

```python
import functools
import math

import jax
import jax.numpy as jnp
from jax import lax
from jax.experimental import pallas as pl
from jax.experimental.pallas import tpu as pltpu

F32 = jnp.float32
BF16 = jnp.bfloat16

GRID_W = 64
HEAD_DIM = 64
N_Q_HEADS = 8
N_KV_HEADS = 2
Q_GROUP = N_Q_HEADS // N_KV_HEADS
ATTN_DIM = N_Q_HEADS * HEAD_DIM
KV_DIM = N_KV_HEADS * HEAD_DIM
CONV_DIM = 512
ROPE_THETA = 10000.0
ROPE_AXIS_DIM = HEAD_DIM // 2
POOL_WINDOWS = (2, 4, 8, 16)
RMS_EPS = 1e-6

LANES = 128
SUBLANES = 8
BF16_ROWS = 16
HALO = SUBLANES
MXU_DIM = 256

VMEM_LIMIT = 56 * 1024 * 1024

TM_PROJ = 512
TM_FFN = 512
TQ_ATTN = 512
ATTN_KEY_BLOCK = 256
ATTN_LOOKAHEAD = 3
FFN_CHUNK = 256


def _rmsnorm(x, g):
    ms = jnp.mean(x * x, axis=-1, keepdims=True)
    return x * lax.rsqrt(ms + RMS_EPS) * g


def _dot(a, b):
    return jnp.dot(a, b, preferred_element_type=F32)


def _const_spec(shape):
    zeros = (0,) * len(shape)
    return pl.BlockSpec(shape, lambda *_: zeros, pipeline_mode=pl.Buffered(1))


def _layer_spec(arr, layer, block=None, pos=None):
    block = tuple(arr.shape[1:]) if block is None else tuple(block)
    pos = (0,) * len(block) if pos is None else tuple(pos)
    index = (layer,) + pos
    return pl.BlockSpec((None,) + block, lambda *_: index, pipeline_mode=pl.Buffered(1))


def _halo_specs(tm, n_rows, d):
    per = tm // HALO
    last = n_rows // HALO - 1
    main = pl.BlockSpec((tm, d), lambda i: (i, 0))
    prev = pl.BlockSpec((HALO, d), lambda i: (jnp.maximum(i * per - 1, 0), 0))
    nxt = pl.BlockSpec((HALO, d), lambda i: (jnp.minimum((i + 1) * per, last), 0))
    return main, prev, nxt


def _load_ext(x_ref, xp_ref, xn_ref, tiles_per_seq):
    pos = pl.program_id(0) % tiles_per_seq
    xp = jnp.where(pos != 0, xp_ref[...], 0.0)
    xn = jnp.where(pos != tiles_per_seq - 1, xn_ref[...], 0.0)
    return jnp.concatenate([x_ref[...], xp, xn], axis=0)


def _shift_rows(pm, prev_row, next_row):
    tm = pm.shape[0]
    r = lax.broadcasted_iota(jnp.int32, (SUBLANES, pm.shape[1]), 0)
    m1 = pltpu.roll(pm, 1, 0)
    p1 = pltpu.roll(pm, tm - 1, 0)
    m1 = jnp.concatenate(
        [jnp.where(r == 0, prev_row, m1[:SUBLANES]), m1[SUBLANES:]], axis=0)
    p1 = jnp.concatenate(
        [p1[:tm - SUBLANES], jnp.where(r == SUBLANES - 1, next_row, p1[tm - SUBLANES:])], axis=0)
    return m1, p1


def _dwconv3_ext(p, cw):
    tm = p.shape[0] - 2 * HALO
    pm = p[:tm]
    m1, p1 = _shift_rows(pm, p[tm + HALO - 1:tm + HALO], p[tm + HALO:tm + HALO + 1])
    return m1 * cw[0:1] + pm * cw[1:2] + p1 * cw[2:3]


def _head_mean_square(v, bd):
    sq = v * v
    hi = sq.astype(BF16)
    lo = (sq - hi.astype(F32)).astype(BF16)
    outs = []
    for c0 in range(0, v.shape[1], MXU_DIM):
        wd = min(MXU_DIM, v.shape[1] - c0)
        m = bd[:wd, :wd]
        outs.append(_dot(hi[:, c0:c0 + wd], m) + _dot(lo[:, c0:c0 + wd], m))
    return outs[0] if len(outs) == 1 else jnp.concatenate(outs, axis=1)


def _rope(v, c, su, sd):
    outs = []
    for s in range(v.shape[1] // LANES):
        vs = v[:, s * LANES:(s + 1) * LANES]
        up = pltpu.roll(vs, ROPE_AXIS_DIM // 2, 1)
        dn = pltpu.roll(vs, LANES - ROPE_AXIS_DIM // 2, 1)
        outs.append(vs * c + up * su + dn * sd)
    return outs[0] if len(outs) == 1 else jnp.concatenate(outs, axis=1)


def _inproj_kernel(x_ref, xp_ref, xn_ref, g_ref, w_ref, qg_ref, kg_ref, bd_ref,
                   c_ref, su_ref, sd_ref, cw_ref,
                   qt_out, ka_out, kb_out, vt_out, conv_out,
                   *, tm, tiles_per_seq, q_scale):
    x_ext = _load_ext(x_ref, xp_ref, xn_ref, tiles_per_seq)
    h_ext = _rmsnorm(x_ext, g_ref[...]).astype(BF16)
    n_head_cols = ATTN_DIM + 2 * KV_DIM + CONV_DIM
    main = _dot(h_ext[:tm], w_ref[:, :n_head_cols])
    gc = _dot(h_ext, w_ref[:, n_head_cols:])

    bd = bd_ref[...]
    c, su, sd = c_ref[...], su_ref[...], sd_ref[...]

    q_raw = main[:, :ATTN_DIM]
    qn = q_raw * lax.rsqrt(_head_mean_square(q_raw, bd) + RMS_EPS) * qg_ref[...]
    qt_out[...] = (_rope(qn, c, su, sd) * q_scale).T.astype(qt_out.dtype)
    vt_out[...] = main[:, ATTN_DIM + KV_DIM:ATTN_DIM + 2 * KV_DIM].T.astype(vt_out.dtype)

    k_raw = main[:, ATTN_DIM:ATTN_DIM + KV_DIM]
    kn = k_raw * lax.rsqrt(_head_mean_square(k_raw, bd) + RMS_EPS) * kg_ref[...]
    kr = _rope(kn, c, su, sd)
    sw = pltpu.roll(kr, HEAD_DIM, 1)
    low = lax.broadcasted_iota(jnp.int32, (tm, LANES), 1) < HEAD_DIM
    ka_out[0] = jnp.where(low, kr, 0.0).astype(ka_out.dtype)
    kb_out[0] = jnp.where(low, 0.0, sw).astype(kb_out.dtype)
    ka_out[1] = jnp.where(low, sw, 0.0).astype(ka_out.dtype)
    kb_out[1] = jnp.where(low, 0.0, kr).astype(kb_out.dtype)

    gate_b = main[:, ATTN_DIM + 2 * KV_DIM:]
    z = gc[:, :CONV_DIM] * gc[:, CONV_DIM:]
    conv_out[...] = (gate_b * _dwconv3_ext(z, cw_ref[...])).astype(conv_out.dtype)


def _inproj(x2d, g_all, g_row, w_in, layer, qg, kg, bd, c_tab, su_tab, sd_tab, sconv_w, *, seq):
    n_rows, d = x2d.shape
    tm = TM_PROJ
    tiles_per_seq = seq // tm
    main, prev, nxt = _halo_specs(tm, n_rows, d)
    tab = pl.BlockSpec((tm, LANES), lambda i: (i % tiles_per_seq, 0))
    k_spec = pl.BlockSpec((N_KV_HEADS, tm, LANES), lambda i: (0, i, 0))
    k_shape = jax.ShapeDtypeStruct((N_KV_HEADS, n_rows, LANES), BF16)
    q_scale = HEAD_DIM ** -0.5 * math.log2(math.e)
    return pl.pallas_call(
        functools.partial(_inproj_kernel, tm=tm, tiles_per_seq=tiles_per_seq, q_scale=q_scale),
        grid=(n_rows // tm,),
        in_specs=[main, prev, nxt, _layer_spec(g_all, g_row), _layer_spec(w_in, layer),
                  _const_spec(qg.shape), _const_spec(kg.shape), _const_spec(bd.shape),
                  tab, tab, tab, _layer_spec(sconv_w, layer)],
        out_specs=[pl.BlockSpec((ATTN_DIM, tm), lambda i: (0, i)),
                   k_spec, k_spec,
                   pl.BlockSpec((KV_DIM, tm), lambda i: (0, i)),
                   pl.BlockSpec((tm, CONV_DIM), lambda i: (i, 0))],
        out_shape=[jax.ShapeDtypeStruct((ATTN_DIM, n_rows), BF16),
                   k_shape, k_shape,
                   jax.ShapeDtypeStruct((KV_DIM, n_rows), BF16),
                   jax.ShapeDtypeStruct((n_rows, CONV_DIM), BF16)],
        compiler_params=pltpu.CompilerParams(
            dimension_semantics=("parallel",), vmem_limit_bytes=VMEM_LIMIT),
        name="inproj",
    )(x2d, x2d, x2d, g_all, w_in, qg, kg, bd, c_tab, su_tab, sd_tab, sconv_w)


def _attn_kernel(qt_ref, ka_ref, kb_ref, vt_ref, o_ref):
    seq = vt_ref.shape[1]
    k_refs = (ka_ref, kb_ref, ka_ref, kb_ref)
    units = [(g, k0) for g in range(Q_GROUP) for k0 in range(0, seq, ATTN_KEY_BLOCK)]

    def scores(u):
        g, k0 = units[u]
        slab = g // 2
        return _dot(k_refs[g][k0:k0 + ATTN_KEY_BLOCK, :],
                    qt_ref[slab * LANES:(slab + 1) * LANES, :])

    pending = [scores(u) for u in range(min(ATTN_LOOKAHEAD, len(units)))]
    outs = []
    m = l = acc = None
    for u, (g, k0) in enumerate(units):
        st = pending.pop(0)
        if u + ATTN_LOOKAHEAD < len(units):
            pending.append(scores(u + ATTN_LOOKAHEAD))
        vt = vt_ref[:, k0:k0 + ATTN_KEY_BLOCK]
        blk_max = jnp.max(st, axis=0, keepdims=True)
        if k0 == 0:
            m = blk_max
            p = jnp.exp2(st - m)
            l = jnp.sum(p, axis=0, keepdims=True)
            acc = _dot(vt, p.astype(BF16))
        else:
            m_new = jnp.maximum(m, blk_max)
            alpha = jnp.exp2(m - m_new)
            p = jnp.exp2(st - m_new)
            l = alpha * l + jnp.sum(p, axis=0, keepdims=True)
            acc = alpha * acc + _dot(vt, p.astype(BF16))
            m = m_new
        if k0 + ATTN_KEY_BLOCK == seq:
            outs.append(acc * (1.0 / l))
    o_ref[...] = jnp.concatenate(outs, axis=0).T.astype(o_ref.dtype)


def _attention(qt, ka, kb, vt, *, batch, seq):
    tq = TQ_ATTN
    q_tiles = seq // tq
    group_cols = Q_GROUP * HEAD_DIM
    k_spec = pl.BlockSpec((None, seq, LANES), lambda b, h, i: (h, b, 0))
    return pl.pallas_call(
        _attn_kernel,
        grid=(batch, N_KV_HEADS, q_tiles),
        in_specs=[pl.BlockSpec((group_cols, tq), lambda b, h, i: (h, b * q_tiles + i)),
                  k_spec, k_spec,
                  pl.BlockSpec((HEAD_DIM, seq), lambda b, h, i: (h, b))],
        out_specs=pl.BlockSpec((tq, group_cols), lambda b, h, i: (b * q_tiles + i, h)),
        out_shape=jax.ShapeDtypeStruct((qt.shape[1], qt.shape[0]), BF16),
        compiler_params=pltpu.CompilerParams(
            dimension_semantics=("parallel", "parallel", "arbitrary"),
            vmem_limit_bytes=VMEM_LIMIT),
        name="attention",
    )(qt, ka, kb, vt)


def _halo16_specs(tm, n_rows, cols):
    per = tm // BF16_ROWS
    last = n_rows // BF16_ROWS - 1
    prev = pl.BlockSpec((BF16_ROWS, cols), lambda i: (jnp.maximum(i * per - 1, 0), 0))
    nxt = pl.BlockSpec((BF16_ROWS, cols), lambda i: (jnp.minimum((i + 1) * per, last), 0))
    return prev, nxt


def _ffn_core(x1_ext, g2, wup_ref, cw_ref, wdn_ref, g3, o_ref, act_ref, x1_ref,
              *, tiles_per_seq, chunk):
    tm, d_ff = act_ref.shape
    pos = pl.program_id(0) % tiles_per_seq
    x1_ref[...] = x1_ext[:tm]
    x1p = jnp.where(pos != 0, x1_ext[tm:tm + HALO], 0.0)
    x1n = jnp.where(pos != tiles_per_seq - 1, x1_ext[tm + HALO:], 0.0)
    h_ext = _rmsnorm(jnp.concatenate([x1_ref[...], x1p, x1n], axis=0), g2).astype(BF16)

    def up_branch(c0):
        p = _dot(h_ext, wup_ref[:, c0:c0 + chunk])
        return _dwconv3_ext(p, cw_ref[:, c0:c0 + chunk])

    for c0 in range(0, d_ff, chunk):
        u = up_branch(c0)
        gt = up_branch(d_ff + c0)
        act = jax.nn.gelu(gt, approximate=True) * u
        act_ref[:, c0:c0 + chunk] = act.astype(act_ref.dtype)
    half = tm // 2
    for r0 in (0, half):
        y = _dot(act_ref[r0:r0 + half], wdn_ref[...])
        o_ref[r0:r0 + half] = x1_ref[r0:r0 + half] + _rmsnorm(y, g3)


def _tail_even_kernel(a_ref, ap_ref, an_ref, c_ref, cp_ref, cn_ref, x_ref, xp_ref, xn_ref,
                      wa_ref, wc_ref, g1_ref, g2_ref, wup_ref, cw_ref, wdn_ref, g3_ref,
                      o_ref, act_ref, x1_ref, *, tiles_per_seq, chunk):
    def ext(m_ref, p_ref, n_ref):
        halo = jnp.concatenate([p_ref[...].astype(F32)[HALO:], n_ref[...].astype(F32)[:HALO]], axis=0)
        return jnp.concatenate([m_ref[...], halo.astype(m_ref.dtype)], axis=0)

    mix = _dot(ext(a_ref, ap_ref, an_ref), wa_ref[...]) + _dot(ext(c_ref, cp_ref, cn_ref), wc_ref[...])
    x_ext = jnp.concatenate([x_ref[...], xp_ref[HALO:], xn_ref[:HALO]], axis=0)
    x1_ext = x_ext + _rmsnorm(mix, g1_ref[...])
    _ffn_core(x1_ext, g2_ref[...], wup_ref, cw_ref, wdn_ref, g3_ref[...], o_ref, act_ref, x1_ref,
              tiles_per_seq=tiles_per_seq, chunk=chunk)


def _tail_odd_kernel(x_ref, xp_ref, xn_ref, g0_ref, pw_ref, ps_ref,
                     g1_ref, g2_ref, wup_ref, cw_ref, wdn_ref, g3_ref,
                     o_ref, act_ref, x1_ref, *, tiles_per_seq, chunk, seq):
    tm, d = x_ref.shape
    pos = pl.program_id(0) % tiles_per_seq
    xp = jnp.where(pos != 0, xp_ref[...], 0.0)
    xn = jnp.where(pos != tiles_per_seq - 1, xn_ref[...], 0.0)
    x_wide = jnp.concatenate([xp, x_ref[...], xn], axis=0)
    h_wide = _rmsnorm(x_wide, g0_ref[...])
    n_wide = tm + 2 * BF16_ROWS
    n_ext = tm + 2 * HALO
    group = d // len(POOL_WINDOWS)

    def reorder(a):
        return jnp.concatenate([a[HALO:HALO + tm], a[:HALO], a[HALO + tm:]], axis=0)

    t = pos * tm - HALO + lax.broadcasted_iota(jnp.int32, (n_ext, LANES), 0)
    outs = []
    for gi, w in enumerate(POOL_WINDOWS):
        hg = h_wide[:, gi * group:(gi + 1) * group]
        acc, span = hg, 1
        while span < w:
            acc = acc + pltpu.roll(acc, span, 0)
            span *= 2
        ahead = w // 2 - 1
        if ahead:
            acc = pltpu.roll(acc, n_wide - ahead, 0)
        win = acc[HALO:HALO + n_ext]
        cnt = jnp.minimum(t + w // 2, seq) - jnp.maximum(t - w // 2, 0)
        inv = 1.0 / jnp.maximum(cnt, 1).astype(F32)
        inv = jnp.concatenate([inv] * (group // LANES), axis=1)
        pooled = win * inv - hg[HALO:HALO + n_ext]
        outs.append(_dot(reorder(pooled).astype(BF16), pw_ref[gi]))
    mix = jnp.concatenate(outs, axis=1) * ps_ref[...]
    x1_ext = reorder(x_wide[HALO:HALO + n_ext]) + _rmsnorm(mix, g1_ref[...])
    _ffn_core(x1_ext, g2_ref[...], wup_ref, cw_ref, wdn_ref, g3_ref[...], o_ref, act_ref, x1_ref,
              tiles_per_seq=tiles_per_seq, chunk=chunk)


def _tail_call(body, row_inputs, row_specs, const_inputs, const_specs, x2d, d_ff, name):
    n_rows, d = x2d.shape
    tm = TM_FFN
    return pl.pallas_call(
        body,
        grid=(n_rows // tm,),
        in_specs=list(row_specs) + list(const_specs),
        out_specs=pl.BlockSpec((tm, d), lambda i: (i, 0)),
        out_shape=jax.ShapeDtypeStruct(x2d.shape, x2d.dtype),
        scratch_shapes=[pltpu.VMEM((tm, d_ff), BF16), pltpu.VMEM((tm, d), F32)],
        compiler_params=pltpu.CompilerParams(
            dimension_semantics=("parallel",), vmem_limit_bytes=VMEM_LIMIT),
        name=name,
    )(*row_inputs, *const_inputs)


def _tail_even(attn, conv, x2d, w_out, e, g_all, g_row, w_up, conv_w, w_down, layer, *, seq):
    n_rows, d = x2d.shape
    tm = TM_FFN
    row = lambda cols: pl.BlockSpec((tm, cols), lambda i: (i, 0))
    a_prev, a_next = _halo16_specs(tm, n_rows, ATTN_DIM)
    c_prev, c_next = _halo16_specs(tm, n_rows, CONV_DIM)
    x_prev, x_next = _halo16_specs(tm, n_rows, d)
    body = functools.partial(_tail_even_kernel, tiles_per_seq=seq // tm, chunk=FFN_CHUNK)
    return _tail_call(
        body,
        (attn, attn, attn, conv, conv, conv, x2d, x2d, x2d),
        (row(ATTN_DIM), a_prev, a_next, row(CONV_DIM), c_prev, c_next, row(d), x_prev, x_next),
        (w_out, w_out, g_all, g_all, w_up, conv_w, w_down, g_all),
        (_layer_spec(w_out, e, (ATTN_DIM, d), (0, 0)),
         _layer_spec(w_out, e, (CONV_DIM, d), (ATTN_DIM // CONV_DIM, 0)),
         _layer_spec(g_all, g_row + 1), _layer_spec(g_all, g_row + 2),
         _layer_spec(w_up, layer), _layer_spec(conv_w, layer), _layer_spec(w_down, layer),
         _layer_spec(g_all, g_row + 3)),
        x2d, w_down.shape[1], "tail_even")


def _tail_odd(x2d, pool_w, pool_scale, o, g_all, g_row, w_up, conv_w, w_down, layer, *, seq):
    n_rows, d = x2d.shape
    tm = TM_FFN
    x_prev, x_next = _halo16_specs(tm, n_rows, d)
    body = functools.partial(_tail_odd_kernel, tiles_per_seq=seq // tm, chunk=FFN_CHUNK, seq=seq)
    return _tail_call(
        body,
        (x2d, x2d, x2d),
        (pl.BlockSpec((tm, d), lambda i: (i, 0)), x_prev, x_next),
        (g_all, pool_w, pool_scale, g_all, g_all, w_up, conv_w, w_down, g_all),
        (_layer_spec(g_all, g_row), _layer_spec(pool_w, o), _layer_spec(pool_scale, o),
         _layer_spec(g_all, g_row + 1), _layer_spec(g_all, g_row + 2),
         _layer_spec(w_up, layer), _layer_spec(conv_w, layer), _layer_spec(w_down, layer),
         _layer_spec(g_all, g_row + 3)),
        x2d, w_down.shape[1], "tail_odd")


def _rope_tables(seq):
    rows = seq // GRID_W
    row = jnp.repeat(jnp.arange(rows, dtype=F32), GRID_W)
    col = jnp.tile(jnp.arange(GRID_W, dtype=F32), rows)
    inv = ROPE_THETA ** (-jnp.arange(0, ROPE_AXIS_DIM, 2, dtype=F32) / ROPE_AXIS_DIM)
    ang_r = row[:, None] * inv[None, :]
    ang_c = col[:, None] * inv[None, :]
    cos_r, sin_r, cos_c, sin_c = jnp.cos(ang_r), jnp.sin(ang_r), jnp.cos(ang_c), jnp.sin(ang_c)
    zero = jnp.zeros_like(sin_r)
    c = jnp.concatenate([cos_r, cos_r, cos_c, cos_c], axis=1)
    su = jnp.concatenate([zero, sin_r, zero, sin_c], axis=1)
    sd = jnp.concatenate([-sin_r, zero, -sin_c, zero], axis=1)
    reps = LANES // HEAD_DIM
    return tuple(jnp.tile(t, (1, reps)) for t in (c, su, sd))


def kernel(x, norm_g, mix_w_in, q_norm_g, k_norm_g, sconv_w, mix_w_out, pool_w, pool_scale,
           ffn_w_up, ffn_conv_w, ffn_w_down):
    batch, seq, d = x.shape
    depth, norms_per_layer = norm_g.shape[:2]
    assert seq % TM_PROJ == 0 and seq % TM_FFN == 0 and seq % TQ_ATTN == 0
    assert ffn_w_down.shape[1] % FFN_CHUNK == 0

    c_tab, su_tab, sd_tab = _rope_tables(seq)
    lane_head = jnp.arange(MXU_DIM) // HEAD_DIM
    bd = jnp.where(lane_head[:, None] == lane_head[None, :], 1.0 / HEAD_DIM, 0.0).astype(BF16)

    w_in, w_out, pool_w_bf = mix_w_in.astype(BF16), mix_w_out.astype(BF16), pool_w.astype(BF16)
    w_up, w_down = ffn_w_up.astype(BF16), ffn_w_down.astype(BF16)
    g_all = norm_g.reshape(depth * norms_per_layer, 1, d)
    pool_scale3 = pool_scale[:, None, :]

    x2d = x.reshape(batch * seq, d)
    for i in range(depth):
        g_row = i * norms_per_layer
        if i % 2 == 0:
            e = i // 2
            qg = jnp.tile(q_norm_g[e], N_Q_HEADS)[None, :]
            kg = jnp.tile(k_norm_g[e], N_KV_HEADS)[None, :]
            qt, ka, kb, vt, conv = _inproj(
                x2d, g_all, g_row, w_in, e, qg, kg, bd, c_tab, su_tab, sd_tab, sconv_w, seq=seq)
            attn = _attention(qt, ka, kb, vt, batch=batch, seq=seq)
            x2d = _tail_even(attn, conv, x2d, w_out, e, g_all, g_row, w_up, ffn_conv_w, w_down, i,
                             seq=seq)
        else:
            x2d = _tail_odd(x2d, pool_w_bf, pool_scale3, i // 2, g_all, g_row, w_up, ffn_conv_w,
                            w_down, i, seq=seq)
    return x2d.reshape(batch, seq, d)
```

```python
import functools
import math

import jax
import jax.numpy as jnp
from jax import lax
from jax.experimental import pallas as pl
from jax.experimental.pallas import tpu as pltpu

F32 = jnp.float32
BF16 = jnp.bfloat16

GRID_W = 64
HEAD_DIM = 64
N_Q_HEADS = 8
N_KV_HEADS = 2
Q_GROUP = N_Q_HEADS // N_KV_HEADS
ATTN_DIM = N_Q_HEADS * HEAD_DIM
KV_DIM = N_KV_HEADS * HEAD_DIM
CONV_DIM = 512
ROPE_THETA = 10000.0
ROPE_AXIS_DIM = HEAD_DIM // 2
POOL_WINDOWS = (2, 4, 8, 16)
RMS_EPS = 1e-6

LANES = 128
SUBLANES = 8
BF16_ROWS = 16
HALO = SUBLANES
MXU_DIM = 256

VMEM_LIMIT = 56 * 1024 * 1024

TM_PROJ = 512
TM_FFN = 512
TQ_ATTN = 512
ATTN_KEY_BLOCK = 256
ATTN_LOOKAHEAD = 3
FFN_CHUNK = 256


def _rmsnorm(x, g):
    ms = jnp.mean(x * x, axis=-1, keepdims=True)
    return x * lax.rsqrt(ms + RMS_EPS) * g


def _dot(a, b):
    return jnp.dot(a, b, preferred_element_type=F32)


def _const_spec(shape):
    zeros = (0,) * len(shape)
    return pl.BlockSpec(shape, lambda *_: zeros, pipeline_mode=pl.Buffered(1))


def _layer_spec(arr, layer, block=None, pos=None):
    block = tuple(arr.shape[1:]) if block is None else tuple(block)
    pos = (0,) * len(block) if pos is None else tuple(pos)
    index = (layer,) + pos
    return pl.BlockSpec((None,) + block, lambda *_: index, pipeline_mode=pl.Buffered(1))


def _halo_specs(tm, n_rows, d):
    per = tm // HALO
    last = n_rows // HALO - 1
    main = pl.BlockSpec((tm, d), lambda i: (i, 0))
    prev = pl.BlockSpec((HALO, d), lambda i: (jnp.maximum(i * per - 1, 0), 0))
    nxt = pl.BlockSpec((HALO, d), lambda i: (jnp.minimum((i + 1) * per, last), 0))
    return main, prev, nxt


def _load_ext(x_ref, xp_ref, xn_ref, tiles_per_seq):
    pos = pl.program_id(0) % tiles_per_seq
    xp = jnp.where(pos != 0, xp_ref[...], 0.0)
    xn = jnp.where(pos != tiles_per_seq - 1, xn_ref[...], 0.0)
    return jnp.concatenate([x_ref[...], xp, xn], axis=0)


def _shift_rows(pm, prev_row, next_row):
    tm = pm.shape[0]
    r = lax.broadcasted_iota(jnp.int32, (SUBLANES, pm.shape[1]), 0)
    m1 = pltpu.roll(pm, 1, 0)
    p1 = pltpu.roll(pm, tm - 1, 0)
    m1 = jnp.concatenate(
        [jnp.where(r == 0, prev_row, m1[:SUBLANES]), m1[SUBLANES:]], axis=0)
    p1 = jnp.concatenate(
        [p1[:tm - SUBLANES], jnp.where(r == SUBLANES - 1, next_row, p1[tm - SUBLANES:])], axis=0)
    return m1, p1


def _dwconv3_ext(p, cw):
    tm = p.shape[0] - 2 * HALO
    pm = p[:tm]
    m1, p1 = _shift_rows(pm, p[tm + HALO - 1:tm + HALO], p[tm + HALO:tm + HALO + 1])
    return m1 * cw[0:1] + pm * cw[1:2] + p1 * cw[2:3]


def _head_mean_square(v, bd):
    sq = v * v
    hi = sq.astype(BF16)
    lo = (sq - hi.astype(F32)).astype(BF16)
    outs = []
    for c0 in range(0, v.shape[1], MXU_DIM):
        wd = min(MXU_DIM, v.shape[1] - c0)
        m = bd[:wd, :wd]
        outs.append(_dot(hi[:, c0:c0 + wd], m) + _dot(lo[:, c0:c0 + wd], m))
    return outs[0] if len(outs) == 1 else jnp.concatenate(outs, axis=1)


def _rope(v, c, su, sd):
    outs = []
    for s in range(v.shape[1] // LANES):
        vs = v[:, s * LANES:(s + 1) * LANES]
        up = pltpu.roll(vs, ROPE_AXIS_DIM // 2, 1)
        dn = pltpu.roll(vs, LANES - ROPE_AXIS_DIM // 2, 1)
        outs.append(vs * c + up * su + dn * sd)
    return outs[0] if len(outs) == 1 else jnp.concatenate(outs, axis=1)


def _inproj_kernel(x_ref, xp_ref, xn_ref, g_ref, w_ref, qg_ref, kg_ref, bd_ref,
                   c_ref, su_ref, sd_ref, cw_ref,
                   qt_out, ka_out, kb_out, vt_out, conv_out,
                   *, tm, tiles_per_seq, q_scale):
    x_ext = _load_ext(x_ref, xp_ref, xn_ref, tiles_per_seq)
    h_ext = _rmsnorm(x_ext, g_ref[...]).astype(BF16)
    n_head_cols = ATTN_DIM + 2 * KV_DIM + CONV_DIM
    main = _dot(h_ext[:tm], w_ref[:, :n_head_cols])
    gc = _dot(h_ext, w_ref[:, n_head_cols:])

    bd = bd_ref[...]
    c, su, sd = c_ref[...], su_ref[...], sd_ref[...]

    q_raw = main[:, :ATTN_DIM]
    qn = q_raw * lax.rsqrt(_head_mean_square(q_raw, bd) + RMS_EPS) * qg_ref[...]
    qt_out[...] = (_rope(qn, c, su, sd) * q_scale).T.astype(qt_out.dtype)
    vt_out[...] = main[:, ATTN_DIM + KV_DIM:ATTN_DIM + 2 * KV_DIM].T.astype(vt_out.dtype)

    k_raw = main[:, ATTN_DIM:ATTN_DIM + KV_DIM]
    kn = k_raw * lax.rsqrt(_head_mean_square(k_raw, bd) + RMS_EPS) * kg_ref[...]
    kr = _rope(kn, c, su, sd)
    sw = pltpu.roll(kr, HEAD_DIM, 1)
    low = lax.broadcasted_iota(jnp.int32, (tm, LANES), 1) < HEAD_DIM
    ka_out[0] = jnp.where(low, kr, 0.0).astype(ka_out.dtype)
    kb_out[0] = jnp.where(low, 0.0, sw).astype(kb_out.dtype)
    ka_out[1] = jnp.where(low, sw, 0.0).astype(ka_out.dtype)
    kb_out[1] = jnp.where(low, 0.0, kr).astype(kb_out.dtype)

    gate_b = main[:, ATTN_DIM + 2 * KV_DIM:]
    z = gc[:, :CONV_DIM] * gc[:, CONV_DIM:]
    conv_out[...] = (gate_b * _dwconv3_ext(z, cw_ref[...])).astype(conv_out.dtype)


def _inproj(x2d, g_all, g_row, w_in, layer, qg, kg, bd, c_tab, su_tab, sd_tab, sconv_w, *, seq):
    n_rows, d = x2d.shape
    tm = TM_PROJ
    tiles_per_seq = seq // tm
    main, prev, nxt = _halo_specs(tm, n_rows, d)
    tab = pl.BlockSpec((tm, LANES), lambda i: (i % tiles_per_seq, 0))
    k_spec = pl.BlockSpec((N_KV_HEADS, tm, LANES), lambda i: (0, i, 0))
    k_shape = jax.ShapeDtypeStruct((N_KV_HEADS, n_rows, LANES), BF16)
    q_scale = HEAD_DIM ** -0.5 * math.log2(math.e)
    return pl.pallas_call(
        functools.partial(_inproj_kernel, tm=tm, tiles_per_seq=tiles_per_seq, q_scale=q_scale),
        grid=(n_rows // tm,),
        in_specs=[main, prev, nxt, _layer_spec(g_all, g_row), _layer_spec(w_in, layer),
                  _const_spec(qg.shape), _const_spec(kg.shape), _const_spec(bd.shape),
                  tab, tab, tab, _layer_spec(sconv_w, layer)],
        out_specs=[pl.BlockSpec((ATTN_DIM, tm), lambda i: (0, i)),
                   k_spec, k_spec,
                   pl.BlockSpec((KV_DIM, tm), lambda i: (0, i)),
                   pl.BlockSpec((tm, CONV_DIM), lambda i: (i, 0))],
        out_shape=[jax.ShapeDtypeStruct((ATTN_DIM, n_rows), BF16),
                   k_shape, k_shape,
                   jax.ShapeDtypeStruct((KV_DIM, n_rows), BF16),
                   jax.ShapeDtypeStruct((n_rows, CONV_DIM), BF16)],
        compiler_params=pltpu.CompilerParams(
            dimension_semantics=("parallel",), vmem_limit_bytes=VMEM_LIMIT),
        name="inproj",
    )(x2d, x2d, x2d, g_all, w_in, qg, kg, bd, c_tab, su_tab, sd_tab, sconv_w)


def _attn_kernel(qt_ref, ka_ref, kb_ref, vt_ref, o_ref):
    seq = vt_ref.shape[1]
    k_refs = (ka_ref, kb_ref, ka_ref, kb_ref)
    units = [(g, k0) for g in range(Q_GROUP) for k0 in range(0, seq, ATTN_KEY_BLOCK)]

    def scores(u):
        g, k0 = units[u]
        slab = g // 2
        return _dot(k_refs[g][k0:k0 + ATTN_KEY_BLOCK, :],
                    qt_ref[slab * LANES:(slab + 1) * LANES, :])

    pending = [scores(u) for u in range(min(ATTN_LOOKAHEAD, len(units)))]
    ones = jnp.ones((BF16_ROWS, ATTN_KEY_BLOCK), BF16)
    outs = []
    m = acc = None
    for u, (g, k0) in enumerate(units):
        st = pending.pop(0)
        if u + ATTN_LOOKAHEAD < len(units):
            pending.append(scores(u + ATTN_LOOKAHEAD))
        vt1 = jnp.concatenate([vt_ref[:, k0:k0 + ATTN_KEY_BLOCK], ones], axis=0)
        blk_max = jnp.max(st, axis=0, keepdims=True)
        if k0 == 0:
            m = blk_max
            acc = _dot(vt1, jnp.exp2(st - m).astype(BF16))
        else:
            m_new = jnp.maximum(m, blk_max)
            acc = jnp.exp2(m - m_new) * acc + _dot(vt1, jnp.exp2(st - m_new).astype(BF16))
            m = m_new
        if k0 + ATTN_KEY_BLOCK == seq:
            outs.append(acc[:HEAD_DIM] * (1.0 / acc[HEAD_DIM:HEAD_DIM + 1]))
    o_ref[...] = jnp.concatenate(outs, axis=0).T.astype(o_ref.dtype)


def _attention(qt, ka, kb, vt, *, batch, seq):
    tq = TQ_ATTN
    q_tiles = seq // tq
    group_cols = Q_GROUP * HEAD_DIM
    k_spec = pl.BlockSpec((None, seq, LANES), lambda b, h, i: (h, b, 0))
    return pl.pallas_call(
        _attn_kernel,
        grid=(batch, N_KV_HEADS, q_tiles),
        in_specs=[pl.BlockSpec((group_cols, tq), lambda b, h, i: (h, b * q_tiles + i)),
                  k_spec, k_spec,
                  pl.BlockSpec((HEAD_DIM, seq), lambda b, h, i: (h, b))],
        out_specs=pl.BlockSpec((tq, group_cols), lambda b, h, i: (b * q_tiles + i, h)),
        out_shape=jax.ShapeDtypeStruct((qt.shape[1], qt.shape[0]), BF16),
        compiler_params=pltpu.CompilerParams(
            dimension_semantics=("parallel", "parallel", "arbitrary"),
            vmem_limit_bytes=VMEM_LIMIT),
        name="attention",
    )(qt, ka, kb, vt)


def _halo16_specs(tm, n_rows, cols):
    per = tm // BF16_ROWS
    last = n_rows // BF16_ROWS - 1
    prev = pl.BlockSpec((BF16_ROWS, cols), lambda i: (jnp.maximum(i * per - 1, 0), 0))
    nxt = pl.BlockSpec((BF16_ROWS, cols), lambda i: (jnp.minimum((i + 1) * per, last), 0))
    return prev, nxt


def _ffn_core(x1_ext, g2, wup_ref, cw_ref, wdn_ref, g3, o_ref, act_ref, x1_ref,
              *, tiles_per_seq, chunk):
    tm, d_ff = act_ref.shape
    pos = pl.program_id(0) % tiles_per_seq
    x1_ref[...] = x1_ext[:tm]
    x1p = jnp.where(pos != 0, x1_ext[tm:tm + HALO], 0.0)
    x1n = jnp.where(pos != tiles_per_seq - 1, x1_ext[tm + HALO:], 0.0)
    h_ext = _rmsnorm(jnp.concatenate([x1_ref[...], x1p, x1n], axis=0), g2).astype(BF16)

    def up_branch(c0):
        p = _dot(h_ext, wup_ref[:, c0:c0 + chunk])
        return _dwconv3_ext(p, cw_ref[:, c0:c0 + chunk])

    for c0 in range(0, d_ff, chunk):
        u = up_branch(c0)
        gt = up_branch(d_ff + c0)
        act = jax.nn.gelu(gt, approximate=True) * u
        act_ref[:, c0:c0 + chunk] = act.astype(act_ref.dtype)
    half = tm // 2
    for r0 in (0, half):
        y = _dot(act_ref[r0:r0 + half], wdn_ref[...])
        o_ref[r0:r0 + half] = x1_ref[r0:r0 + half] + _rmsnorm(y, g3)


def _tail_even_kernel(a_ref, ap_ref, an_ref, c_ref, cp_ref, cn_ref, x_ref, xp_ref, xn_ref,
                      wa_ref, wc_ref, g1_ref, g2_ref, wup_ref, cw_ref, wdn_ref, g3_ref,
                      o_ref, act_ref, x1_ref, *, tiles_per_seq, chunk):
    def ext(m_ref, p_ref, n_ref):
        halo = jnp.concatenate([p_ref[...].astype(F32)[HALO:], n_ref[...].astype(F32)[:HALO]], axis=0)
        return jnp.concatenate([m_ref[...], halo.astype(m_ref.dtype)], axis=0)

    mix = _dot(ext(a_ref, ap_ref, an_ref), wa_ref[...]) + _dot(ext(c_ref, cp_ref, cn_ref), wc_ref[...])
    x_ext = jnp.concatenate([x_ref[...], xp_ref[HALO:], xn_ref[:HALO]], axis=0)
    x1_ext = x_ext + _rmsnorm(mix, g1_ref[...])
    _ffn_core(x1_ext, g2_ref[...], wup_ref, cw_ref, wdn_ref, g3_ref[...], o_ref, act_ref, x1_ref,
              tiles_per_seq=tiles_per_seq, chunk=chunk)


def _tail_odd_kernel(x_ref, xp_ref, xn_ref, g0_ref, pw_ref, ps_ref,
                     g1_ref, g2_ref, wup_ref, cw_ref, wdn_ref, g3_ref,
                     o_ref, act_ref, x1_ref, *, tiles_per_seq, chunk, seq):
    tm, d = x_ref.shape
    pos = pl.program_id(0) % tiles_per_seq
    xp = jnp.where(pos != 0, xp_ref[...], 0.0)
    xn = jnp.where(pos != tiles_per_seq - 1, xn_ref[...], 0.0)
    x_wide = jnp.concatenate([xp, x_ref[...], xn], axis=0)
    h_wide = _rmsnorm(x_wide, g0_ref[...])
    n_wide = tm + 2 * BF16_ROWS
    n_ext = tm + 2 * HALO
    group = d // len(POOL_WINDOWS)

    def reorder(a):
        return jnp.concatenate([a[HALO:HALO + tm], a[:HALO], a[HALO + tm:]], axis=0)

    t = pos * tm - HALO + lax.broadcasted_iota(jnp.int32, (n_ext, LANES), 0)
    outs = []
    for gi, w in enumerate(POOL_WINDOWS):
        hg = h_wide[:, gi * group:(gi + 1) * group]
        acc, span = hg, 1
        while span < w:
            acc = acc + pltpu.roll(acc, span, 0)
            span *= 2
        ahead = w // 2 - 1
        if ahead:
            acc = pltpu.roll(acc, n_wide - ahead, 0)
        win = acc[HALO:HALO + n_ext]
        cnt = jnp.minimum(t + w // 2, seq) - jnp.maximum(t - w // 2, 0)
        inv = 1.0 / jnp.maximum(cnt, 1).astype(F32)
        inv = jnp.concatenate([inv] * (group // LANES), axis=1)
        pooled = win * inv - hg[HALO:HALO + n_ext]
        outs.append(_dot(reorder(pooled).astype(BF16), pw_ref[gi]))
    mix = jnp.concatenate(outs, axis=1) * ps_ref[...]
    x1_ext = reorder(x_wide[HALO:HALO + n_ext]) + _rmsnorm(mix, g1_ref[...])
    _ffn_core(x1_ext, g2_ref[...], wup_ref, cw_ref, wdn_ref, g3_ref[...], o_ref, act_ref, x1_ref,
              tiles_per_seq=tiles_per_seq, chunk=chunk)


def _tail_call(body, row_inputs, row_specs, const_inputs, const_specs, x2d, d_ff, name):
    n_rows, d = x2d.shape
    tm = TM_FFN
    return pl.pallas_call(
        body,
        grid=(n_rows // tm,),
        in_specs=list(row_specs) + list(const_specs),
        out_specs=pl.BlockSpec((tm, d), lambda i: (i, 0)),
        out_shape=jax.ShapeDtypeStruct(x2d.shape, x2d.dtype),
        scratch_shapes=[pltpu.VMEM((tm, d_ff), BF16), pltpu.VMEM((tm, d), F32)],
        compiler_params=pltpu.CompilerParams(
            dimension_semantics=("parallel",), vmem_limit_bytes=VMEM_LIMIT),
        name=name,
    )(*row_inputs, *const_inputs)


def _tail_even(attn, conv, x2d, w_out, e, g_all, g_row, w_up, conv_w, w_down, layer, *, seq):
    n_rows, d = x2d.shape
    tm = TM_FFN
    row = lambda cols: pl.BlockSpec((tm, cols), lambda i: (i, 0))
    a_prev, a_next = _halo16_specs(tm, n_rows, ATTN_DIM)
    c_prev, c_next = _halo16_specs(tm, n_rows, CONV_DIM)
    x_prev, x_next = _halo16_specs(tm, n_rows, d)
    body = functools.partial(_tail_even_kernel, tiles_per_seq=seq // tm, chunk=FFN_CHUNK)
    return _tail_call(
        body,
        (attn, attn, attn, conv, conv, conv, x2d, x2d, x2d),
        (row(ATTN_DIM), a_prev, a_next, row(CONV_DIM), c_prev, c_next, row(d), x_prev, x_next),
        (w_out, w_out, g_all, g_all, w_up, conv_w, w_down, g_all),
        (_layer_spec(w_out, e, (ATTN_DIM, d), (0, 0)),
         _layer_spec(w_out, e, (CONV_DIM, d), (ATTN_DIM // CONV_DIM, 0)),
         _layer_spec(g_all, g_row + 1), _layer_spec(g_all, g_row + 2),
         _layer_spec(w_up, layer), _layer_spec(conv_w, layer), _layer_spec(w_down, layer),
         _layer_spec(g_all, g_row + 3)),
        x2d, w_down.shape[1], "tail_even")


def _tail_odd(x2d, pool_w, pool_scale, o, g_all, g_row, w_up, conv_w, w_down, layer, *, seq):
    n_rows, d = x2d.shape
    tm = TM_FFN
    x_prev, x_next = _halo16_specs(tm, n_rows, d)
    body = functools.partial(_tail_odd_kernel, tiles_per_seq=seq // tm, chunk=FFN_CHUNK, seq=seq)
    return _tail_call(
        body,
        (x2d, x2d, x2d),
        (pl.BlockSpec((tm, d), lambda i: (i, 0)), x_prev, x_next),
        (g_all, pool_w, pool_scale, g_all, g_all, w_up, conv_w, w_down, g_all),
        (_layer_spec(g_all, g_row), _layer_spec(pool_w, o), _layer_spec(pool_scale, o),
         _layer_spec(g_all, g_row + 1), _layer_spec(g_all, g_row + 2),
         _layer_spec(w_up, layer), _layer_spec(conv_w, layer), _layer_spec(w_down, layer),
         _layer_spec(g_all, g_row + 3)),
        x2d, w_down.shape[1], "tail_odd")


def _rope_tables(seq):
    rows = seq // GRID_W
    row = jnp.repeat(jnp.arange(rows, dtype=F32), GRID_W)
    col = jnp.tile(jnp.arange(GRID_W, dtype=F32), rows)
    inv = ROPE_THETA ** (-jnp.arange(0, ROPE_AXIS_DIM, 2, dtype=F32) / ROPE_AXIS_DIM)
    ang_r = row[:, None] * inv[None, :]
    ang_c = col[:, None] * inv[None, :]
    cos_r, sin_r, cos_c, sin_c = jnp.cos(ang_r), jnp.sin(ang_r), jnp.cos(ang_c), jnp.sin(ang_c)
    zero = jnp.zeros_like(sin_r)
    c = jnp.concatenate([cos_r, cos_r, cos_c, cos_c], axis=1)
    su = jnp.concatenate([zero, sin_r, zero, sin_c], axis=1)
    sd = jnp.concatenate([-sin_r, zero, -sin_c, zero], axis=1)
    reps = LANES // HEAD_DIM
    return tuple(jnp.tile(t, (1, reps)) for t in (c, su, sd))


def kernel(x, norm_g, mix_w_in, q_norm_g, k_norm_g, sconv_w, mix_w_out, pool_w, pool_scale,
           ffn_w_up, ffn_conv_w, ffn_w_down):
    batch, seq, d = x.shape
    depth, norms_per_layer = norm_g.shape[:2]
    assert seq % TM_PROJ == 0 and seq % TM_FFN == 0 and seq % TQ_ATTN == 0
    assert ffn_w_down.shape[1] % FFN_CHUNK == 0

    c_tab, su_tab, sd_tab = _rope_tables(seq)
    lane_head = jnp.arange(MXU_DIM) // HEAD_DIM
    bd = jnp.where(lane_head[:, None] == lane_head[None, :], 1.0 / HEAD_DIM, 0.0).astype(BF16)

    w_in, w_out, pool_w_bf = mix_w_in.astype(BF16), mix_w_out.astype(BF16), pool_w.astype(BF16)
    w_up, w_down = ffn_w_up.astype(BF16), ffn_w_down.astype(BF16)
    g_all = norm_g.reshape(depth * norms_per_layer, 1, d)
    pool_scale3 = pool_scale[:, None, :]

    x2d = x.reshape(batch * seq, d)
    for i in range(depth):
        g_row = i * norms_per_layer
        if i % 2 == 0:
            e = i // 2
            qg = jnp.tile(q_norm_g[e], N_Q_HEADS)[None, :]
            kg = jnp.tile(k_norm_g[e], N_KV_HEADS)[None, :]
            qt, ka, kb, vt, conv = _inproj(
                x2d, g_all, g_row, w_in, e, qg, kg, bd, c_tab, su_tab, sd_tab, sconv_w, seq=seq)
            attn = _attention(qt, ka, kb, vt, batch=batch, seq=seq)
            x2d = _tail_even(attn, conv, x2d, w_out, e, g_all, g_row, w_up, ffn_conv_w, w_down, i,
                             seq=seq)
        else:
            x2d = _tail_odd(x2d, pool_w_bf, pool_scale3, i // 2, g_all, g_row, w_up, ffn_conv_w,
                            w_down, i, seq=seq)
    return x2d.reshape(batch, seq, d)
```

```python
import functools
import math

import jax
import jax.numpy as jnp
from jax import lax
from jax.experimental import pallas as pl
from jax.experimental.pallas import tpu as pltpu

F32 = jnp.float32
BF16 = jnp.bfloat16

GRID_W = 64
HEAD_DIM = 64
N_Q_HEADS = 8
N_KV_HEADS = 2
Q_GROUP = N_Q_HEADS // N_KV_HEADS
ATTN_DIM = N_Q_HEADS * HEAD_DIM
KV_DIM = N_KV_HEADS * HEAD_DIM
CONV_DIM = 512
ROPE_THETA = 10000.0
ROPE_AXIS_DIM = HEAD_DIM // 2
POOL_WINDOWS = (2, 4, 8, 16)
RMS_EPS = 1e-6

LANES = 128
SUBLANES = 8
BF16_ROWS = 16
HALO = SUBLANES
MXU_DIM = 256

VMEM_LIMIT = 56 * 1024 * 1024

TM_PROJ = 512
TM_FFN = 512
TQ_ATTN = 512
ATTN_KEY_BLOCK = 256
ATTN_Q_BLOCK = 256
ATTN_LOOKAHEAD = 6
FFN_CHUNK = 256


def _rmsnorm(x, g):
    ms = jnp.mean(x * x, axis=-1, keepdims=True)
    return x * lax.rsqrt(ms + RMS_EPS) * g


def _dot(a, b):
    return jnp.dot(a, b, preferred_element_type=F32)


def _const_spec(shape):
    zeros = (0,) * len(shape)
    return pl.BlockSpec(shape, lambda *_: zeros, pipeline_mode=pl.Buffered(1))


def _layer_spec(arr, layer, block=None, pos=None):
    block = tuple(arr.shape[1:]) if block is None else tuple(block)
    pos = (0,) * len(block) if pos is None else tuple(pos)
    index = (layer,) + pos
    return pl.BlockSpec((None,) + block, lambda *_: index, pipeline_mode=pl.Buffered(1))


def _halo_specs(tm, n_rows, d):
    per = tm // HALO
    last = n_rows // HALO - 1
    main = pl.BlockSpec((tm, d), lambda i: (i, 0))
    prev = pl.BlockSpec((HALO, d), lambda i: (jnp.maximum(i * per - 1, 0), 0))
    nxt = pl.BlockSpec((HALO, d), lambda i: (jnp.minimum((i + 1) * per, last), 0))
    return main, prev, nxt


def _load_ext(x_ref, xp_ref, xn_ref, tiles_per_seq):
    pos = pl.program_id(0) % tiles_per_seq
    xp = jnp.where(pos != 0, xp_ref[...], 0.0)
    xn = jnp.where(pos != tiles_per_seq - 1, xn_ref[...], 0.0)
    return jnp.concatenate([x_ref[...], xp, xn], axis=0)


def _shift_rows(pm, prev_row, next_row):
    tm = pm.shape[0]
    r = lax.broadcasted_iota(jnp.int32, (SUBLANES, pm.shape[1]), 0)
    m1 = pltpu.roll(pm, 1, 0)
    p1 = pltpu.roll(pm, tm - 1, 0)
    m1 = jnp.concatenate(
        [jnp.where(r == 0, prev_row, m1[:SUBLANES]), m1[SUBLANES:]], axis=0)
    p1 = jnp.concatenate(
        [p1[:tm - SUBLANES], jnp.where(r == SUBLANES - 1, next_row, p1[tm - SUBLANES:])], axis=0)
    return m1, p1


def _dwconv3_ext(p, cw):
    tm = p.shape[0] - 2 * HALO
    pm = p[:tm]
    m1, p1 = _shift_rows(pm, p[tm + HALO - 1:tm + HALO], p[tm + HALO:tm + HALO + 1])
    return m1 * cw[0:1] + pm * cw[1:2] + p1 * cw[2:3]


def _head_mean_square(v, bd):
    sq = v * v
    hi = sq.astype(BF16)
    lo = (sq - hi.astype(F32)).astype(BF16)
    outs = []
    for c0 in range(0, v.shape[1], MXU_DIM):
        wd = min(MXU_DIM, v.shape[1] - c0)
        m = bd[:wd, :wd]
        outs.append(_dot(hi[:, c0:c0 + wd], m) + _dot(lo[:, c0:c0 + wd], m))
    return outs[0] if len(outs) == 1 else jnp.concatenate(outs, axis=1)


def _rope(v, c, su, sd):
    outs = []
    for s in range(v.shape[1] // LANES):
        vs = v[:, s * LANES:(s + 1) * LANES]
        up = pltpu.roll(vs, ROPE_AXIS_DIM // 2, 1)
        dn = pltpu.roll(vs, LANES - ROPE_AXIS_DIM // 2, 1)
        outs.append(vs * c + up * su + dn * sd)
    return outs[0] if len(outs) == 1 else jnp.concatenate(outs, axis=1)


def _inproj_kernel(x_ref, xp_ref, xn_ref, g_ref, w_ref, qg_ref, kg_ref, bd_ref,
                   c_ref, su_ref, sd_ref, cw_ref,
                   qt_out, ka_out, kb_out, vt_out, conv_out,
                   *, tm, tiles_per_seq, q_scale):
    x_ext = _load_ext(x_ref, xp_ref, xn_ref, tiles_per_seq)
    h_ext = _rmsnorm(x_ext, g_ref[...]).astype(BF16)
    n_head_cols = ATTN_DIM + 2 * KV_DIM + CONV_DIM
    main = _dot(h_ext[:tm], w_ref[:, :n_head_cols])
    gc = _dot(h_ext, w_ref[:, n_head_cols:])

    bd = bd_ref[...]
    c, su, sd = c_ref[...], su_ref[...], sd_ref[...]

    q_raw = main[:, :ATTN_DIM]
    qn = q_raw * lax.rsqrt(_head_mean_square(q_raw, bd) + RMS_EPS) * qg_ref[...]
    qt_out[...] = (_rope(qn, c, su, sd) * q_scale).T.astype(qt_out.dtype)
    vt_out[...] = main[:, ATTN_DIM + KV_DIM:ATTN_DIM + 2 * KV_DIM].T.astype(vt_out.dtype)

    k_raw = main[:, ATTN_DIM:ATTN_DIM + KV_DIM]
    kn = k_raw * lax.rsqrt(_head_mean_square(k_raw, bd) + RMS_EPS) * kg_ref[...]
    kr = _rope(kn, c, su, sd)
    sw = pltpu.roll(kr, HEAD_DIM, 1)
    low = lax.broadcasted_iota(jnp.int32, (tm, LANES), 1) < HEAD_DIM
    ka_out[0] = jnp.where(low, kr, 0.0).astype(ka_out.dtype)
    kb_out[0] = jnp.where(low, 0.0, sw).astype(kb_out.dtype)
    ka_out[1] = jnp.where(low, sw, 0.0).astype(ka_out.dtype)
    kb_out[1] = jnp.where(low, 0.0, kr).astype(kb_out.dtype)

    gate_b = main[:, ATTN_DIM + 2 * KV_DIM:]
    z = gc[:, :CONV_DIM] * gc[:, CONV_DIM:]
    conv_out[...] = (gate_b * _dwconv3_ext(z, cw_ref[...])).astype(conv_out.dtype)


def _inproj(x2d, g_all, g_row, w_in, layer, qg, kg, bd, c_tab, su_tab, sd_tab, sconv_w, *, seq):
    n_rows, d = x2d.shape
    tm = TM_PROJ
    tiles_per_seq = seq // tm
    main, prev, nxt = _halo_specs(tm, n_rows, d)
    tab = pl.BlockSpec((tm, LANES), lambda i: (i % tiles_per_seq, 0))
    k_spec = pl.BlockSpec((N_KV_HEADS, tm, LANES), lambda i: (0, i, 0))
    k_shape = jax.ShapeDtypeStruct((N_KV_HEADS, n_rows, LANES), BF16)
    q_scale = HEAD_DIM ** -0.5 * math.log2(math.e)
    return pl.pallas_call(
        functools.partial(_inproj_kernel, tm=tm, tiles_per_seq=tiles_per_seq, q_scale=q_scale),
        grid=(n_rows // tm,),
        in_specs=[main, prev, nxt, _layer_spec(g_all, g_row), _layer_spec(w_in, layer),
                  _const_spec(qg.shape), _const_spec(kg.shape), _const_spec(bd.shape),
                  tab, tab, tab, _layer_spec(sconv_w, layer)],
        out_specs=[pl.BlockSpec((ATTN_DIM, tm), lambda i: (0, i)),
                   k_spec, k_spec,
                   pl.BlockSpec((KV_DIM, tm), lambda i: (0, i)),
                   pl.BlockSpec((tm, CONV_DIM), lambda i: (i, 0))],
        out_shape=[jax.ShapeDtypeStruct((ATTN_DIM, n_rows), BF16),
                   k_shape, k_shape,
                   jax.ShapeDtypeStruct((KV_DIM, n_rows), BF16),
                   jax.ShapeDtypeStruct((n_rows, CONV_DIM), BF16)],
        compiler_params=pltpu.CompilerParams(
            dimension_semantics=("parallel",), vmem_limit_bytes=VMEM_LIMIT),
        name="inproj",
    )(x2d, x2d, x2d, g_all, w_in, qg, kg, bd, c_tab, su_tab, sd_tab, sconv_w)


def _attn_kernel(qt_ref, ka_ref, kb_ref, vt_ref, o_ref):
    seq = vt_ref.shape[1]
    tq = qt_ref.shape[1]
    k_refs = (ka_ref, kb_ref, ka_ref, kb_ref)
    q_cols = [slice(c0, c0 + ATTN_Q_BLOCK) for c0 in range(0, tq, ATTN_Q_BLOCK)]
    units = [(g, k0, qi) for g in range(Q_GROUP) for k0 in range(0, seq, ATTN_KEY_BLOCK)
             for qi in range(len(q_cols))]

    def scores(u):
        g, k0, qi = units[u]
        slab = g // 2
        return _dot(k_refs[g][k0:k0 + ATTN_KEY_BLOCK, :],
                    qt_ref[slab * LANES:(slab + 1) * LANES, q_cols[qi]])

    pending = [scores(u) for u in range(min(ATTN_LOOKAHEAD, len(units)))]
    ones = jnp.ones((BF16_ROWS, ATTN_KEY_BLOCK), BF16)
    outs = [[None] * len(q_cols) for _ in range(Q_GROUP)]
    m = [None] * len(q_cols)
    acc = [None] * len(q_cols)
    for u, (g, k0, qi) in enumerate(units):
        st = pending.pop(0)
        if u + ATTN_LOOKAHEAD < len(units):
            pending.append(scores(u + ATTN_LOOKAHEAD))
        vt1 = jnp.concatenate([vt_ref[:, k0:k0 + ATTN_KEY_BLOCK], ones], axis=0)
        blk_max = jnp.max(st, axis=0, keepdims=True)
        if k0 == 0:
            m[qi] = blk_max
            acc[qi] = _dot(vt1, jnp.exp2(st - blk_max).astype(BF16))
        else:
            m_new = jnp.maximum(m[qi], blk_max)
            acc[qi] = (jnp.exp2(m[qi] - m_new) * acc[qi]
                       + _dot(vt1, jnp.exp2(st - m_new).astype(BF16)))
            m[qi] = m_new
        if k0 + ATTN_KEY_BLOCK == seq:
            outs[g][qi] = acc[qi][:HEAD_DIM] * (1.0 / acc[qi][HEAD_DIM:HEAD_DIM + 1])
    o_t = jnp.concatenate([jnp.concatenate(row, axis=1) for row in outs], axis=0)
    o_ref[...] = o_t.T.astype(o_ref.dtype)


def _attention(qt, ka, kb, vt, *, batch, seq):
    tq = TQ_ATTN
    q_tiles = seq // tq
    group_cols = Q_GROUP * HEAD_DIM
    k_spec = pl.BlockSpec((None, seq, LANES), lambda b, h, i: (h, b, 0))
    return pl.pallas_call(
        _attn_kernel,
        grid=(batch, N_KV_HEADS, q_tiles),
        in_specs=[pl.BlockSpec((group_cols, tq), lambda b, h, i: (h, b * q_tiles + i)),
                  k_spec, k_spec,
                  pl.BlockSpec((HEAD_DIM, seq), lambda b, h, i: (h, b))],
        out_specs=pl.BlockSpec((tq, group_cols), lambda b, h, i: (b * q_tiles + i, h)),
        out_shape=jax.ShapeDtypeStruct((qt.shape[1], qt.shape[0]), BF16),
        compiler_params=pltpu.CompilerParams(
            dimension_semantics=("parallel", "parallel", "arbitrary"),
            vmem_limit_bytes=VMEM_LIMIT),
        name="attention",
    )(qt, ka, kb, vt)


def _halo16_specs(tm, n_rows, cols):
    per = tm // BF16_ROWS
    last = n_rows // BF16_ROWS - 1
    prev = pl.BlockSpec((BF16_ROWS, cols), lambda i: (jnp.maximum(i * per - 1, 0), 0))
    nxt = pl.BlockSpec((BF16_ROWS, cols), lambda i: (jnp.minimum((i + 1) * per, last), 0))
    return prev, nxt


def _ffn_core(x1_ext, g2, wup_ref, cw_ref, wdn_ref, g3, o_ref, act_ref, x1_ref,
              *, tiles_per_seq, chunk):
    tm, d_ff = act_ref.shape
    pos = pl.program_id(0) % tiles_per_seq
    x1_ref[...] = x1_ext[:tm]
    x1p = jnp.where(pos != 0, x1_ext[tm:tm + HALO], 0.0)
    x1n = jnp.where(pos != tiles_per_seq - 1, x1_ext[tm + HALO:], 0.0)
    h_ext = _rmsnorm(jnp.concatenate([x1_ref[...], x1p, x1n], axis=0), g2).astype(BF16)

    def up_branch(c0):
        p = _dot(h_ext, wup_ref[:, c0:c0 + chunk])
        return _dwconv3_ext(p, cw_ref[:, c0:c0 + chunk])

    for c0 in range(0, d_ff, chunk):
        u = up_branch(c0)
        gt = up_branch(d_ff + c0)
        act = jax.nn.gelu(gt, approximate=True) * u
        act_ref[:, c0:c0 + chunk] = act.astype(act_ref.dtype)
    half = tm // 2
    for r0 in (0, half):
        y = _dot(act_ref[r0:r0 + half], wdn_ref[...])
        o_ref[r0:r0 + half] = x1_ref[r0:r0 + half] + _rmsnorm(y, g3)


def _tail_even_kernel(a_ref, ap_ref, an_ref, c_ref, cp_ref, cn_ref, x_ref, xp_ref, xn_ref,
                      wa_ref, wc_ref, g1_ref, g2_ref, wup_ref, cw_ref, wdn_ref, g3_ref,
                      o_ref, act_ref, x1_ref, *, tiles_per_seq, chunk):
    def ext(m_ref, p_ref, n_ref):
        halo = jnp.concatenate([p_ref[...].astype(F32)[HALO:], n_ref[...].astype(F32)[:HALO]], axis=0)
        return jnp.concatenate([m_ref[...], halo.astype(m_ref.dtype)], axis=0)

    mix = _dot(ext(a_ref, ap_ref, an_ref), wa_ref[...]) + _dot(ext(c_ref, cp_ref, cn_ref), wc_ref[...])
    x_ext = jnp.concatenate([x_ref[...], xp_ref[HALO:], xn_ref[:HALO]], axis=0)
    x1_ext = x_ext + _rmsnorm(mix, g1_ref[...])
    _ffn_core(x1_ext, g2_ref[...], wup_ref, cw_ref, wdn_ref, g3_ref[...], o_ref, act_ref, x1_ref,
              tiles_per_seq=tiles_per_seq, chunk=chunk)


def _tail_odd_kernel(x_ref, xp_ref, xn_ref, g0_ref, pw_ref, ps_ref,
                     g1_ref, g2_ref, wup_ref, cw_ref, wdn_ref, g3_ref,
                     o_ref, act_ref, x1_ref, *, tiles_per_seq, chunk, seq):
    tm, d = x_ref.shape
    pos = pl.program_id(0) % tiles_per_seq
    xp = jnp.where(pos != 0, xp_ref[...], 0.0)
    xn = jnp.where(pos != tiles_per_seq - 1, xn_ref[...], 0.0)
    x_wide = jnp.concatenate([xp, x_ref[...], xn], axis=0)
    h_wide = _rmsnorm(x_wide, g0_ref[...])
    n_wide = tm + 2 * BF16_ROWS
    n_ext = tm + 2 * HALO
    group = d // len(POOL_WINDOWS)

    def reorder(a):
        return jnp.concatenate([a[HALO:HALO + tm], a[:HALO], a[HALO + tm:]], axis=0)

    t = pos * tm - HALO + lax.broadcasted_iota(jnp.int32, (n_ext, LANES), 0)
    outs = []
    for gi, w in enumerate(POOL_WINDOWS):
        hg = h_wide[:, gi * group:(gi + 1) * group]
        acc, span = hg, 1
        while span < w:
            acc = acc + pltpu.roll(acc, span, 0)
            span *= 2
        ahead = w // 2 - 1
        if ahead:
            acc = pltpu.roll(acc, n_wide - ahead, 0)
        win = acc[HALO:HALO + n_ext]
        cnt = jnp.minimum(t + w // 2, seq) - jnp.maximum(t - w // 2, 0)
        inv = 1.0 / jnp.maximum(cnt, 1).astype(F32)
        inv = jnp.concatenate([inv] * (group // LANES), axis=1)
        pooled = win * inv - hg[HALO:HALO + n_ext]
        outs.append(_dot(reorder(pooled).astype(BF16), pw_ref[gi]))
    mix = jnp.concatenate(outs, axis=1) * ps_ref[...]
    x1_ext = reorder(x_wide[HALO:HALO + n_ext]) + _rmsnorm(mix, g1_ref[...])
    _ffn_core(x1_ext, g2_ref[...], wup_ref, cw_ref, wdn_ref, g3_ref[...], o_ref, act_ref, x1_ref,
              tiles_per_seq=tiles_per_seq, chunk=chunk)


def _tail_call(body, row_inputs, row_specs, const_inputs, const_specs, x2d, d_ff, name):
    n_rows, d = x2d.shape
    tm = TM_FFN
    return pl.pallas_call(
        body,
        grid=(n_rows // tm,),
        in_specs=list(row_specs) + list(const_specs),
        out_specs=pl.BlockSpec((tm, d), lambda i: (i, 0)),
        out_shape=jax.ShapeDtypeStruct(x2d.shape, x2d.dtype),
        scratch_shapes=[pltpu.VMEM((tm, d_ff), BF16), pltpu.VMEM((tm, d), F32)],
        compiler_params=pltpu.CompilerParams(
            dimension_semantics=("parallel",), vmem_limit_bytes=VMEM_LIMIT),
        name=name,
    )(*row_inputs, *const_inputs)


def _tail_even(attn, conv, x2d, w_out, e, g_all, g_row, w_up, conv_w, w_down, layer, *, seq):
    n_rows, d = x2d.shape
    tm = TM_FFN
    row = lambda cols: pl.BlockSpec((tm, cols), lambda i: (i, 0))
    a_prev, a_next = _halo16_specs(tm, n_rows, ATTN_DIM)
    c_prev, c_next = _halo16_specs(tm, n_rows, CONV_DIM)
    x_prev, x_next = _halo16_specs(tm, n_rows, d)
    body = functools.partial(_tail_even_kernel, tiles_per_seq=seq // tm, chunk=FFN_CHUNK)
    return _tail_call(
        body,
        (attn, attn, attn, conv, conv, conv, x2d, x2d, x2d),
        (row(ATTN_DIM), a_prev, a_next, row(CONV_DIM), c_prev, c_next, row(d), x_prev, x_next),
        (w_out, w_out, g_all, g_all, w_up, conv_w, w_down, g_all),
        (_layer_spec(w_out, e, (ATTN_DIM, d), (0, 0)),
         _layer_spec(w_out, e, (CONV_DIM, d), (ATTN_DIM // CONV_DIM, 0)),
         _layer_spec(g_all, g_row + 1), _layer_spec(g_all, g_row + 2),
         _layer_spec(w_up, layer), _layer_spec(conv_w, layer), _layer_spec(w_down, layer),
         _layer_spec(g_all, g_row + 3)),
        x2d, w_down.shape[1], "tail_even")


def _tail_odd(x2d, pool_w, pool_scale, o, g_all, g_row, w_up, conv_w, w_down, layer, *, seq):
    n_rows, d = x2d.shape
    tm = TM_FFN
    x_prev, x_next = _halo16_specs(tm, n_rows, d)
    body = functools.partial(_tail_odd_kernel, tiles_per_seq=seq // tm, chunk=FFN_CHUNK, seq=seq)
    return _tail_call(
        body,
        (x2d, x2d, x2d),
        (pl.BlockSpec((tm, d), lambda i: (i, 0)), x_prev, x_next),
        (g_all, pool_w, pool_scale, g_all, g_all, w_up, conv_w, w_down, g_all),
        (_layer_spec(g_all, g_row), _layer_spec(pool_w, o), _layer_spec(pool_scale, o),
         _layer_spec(g_all, g_row + 1), _layer_spec(g_all, g_row + 2),
         _layer_spec(w_up, layer), _layer_spec(conv_w, layer), _layer_spec(w_down, layer),
         _layer_spec(g_all, g_row + 3)),
        x2d, w_down.shape[1], "tail_odd")


def _rope_tables(seq):
    rows = seq // GRID_W
    row = jnp.repeat(jnp.arange(rows, dtype=F32), GRID_W)
    col = jnp.tile(jnp.arange(GRID_W, dtype=F32), rows)
    inv = ROPE_THETA ** (-jnp.arange(0, ROPE_AXIS_DIM, 2, dtype=F32) / ROPE_AXIS_DIM)
    ang_r = row[:, None] * inv[None, :]
    ang_c = col[:, None] * inv[None, :]
    cos_r, sin_r, cos_c, sin_c = jnp.cos(ang_r), jnp.sin(ang_r), jnp.cos(ang_c), jnp.sin(ang_c)
    zero = jnp.zeros_like(sin_r)
    c = jnp.concatenate([cos_r, cos_r, cos_c, cos_c], axis=1)
    su = jnp.concatenate([zero, sin_r, zero, sin_c], axis=1)
    sd = jnp.concatenate([-sin_r, zero, -sin_c, zero], axis=1)
    reps = LANES // HEAD_DIM
    return tuple(jnp.tile(t, (1, reps)) for t in (c, su, sd))


def kernel(x, norm_g, mix_w_in, q_norm_g, k_norm_g, sconv_w, mix_w_out, pool_w, pool_scale,
           ffn_w_up, ffn_conv_w, ffn_w_down):
    batch, seq, d = x.shape
    depth, norms_per_layer = norm_g.shape[:2]
    assert seq % TM_PROJ == 0 and seq % TM_FFN == 0 and seq % TQ_ATTN == 0
    assert ffn_w_down.shape[1] % FFN_CHUNK == 0

    c_tab, su_tab, sd_tab = _rope_tables(seq)
    lane_head = jnp.arange(MXU_DIM) // HEAD_DIM
    bd = jnp.where(lane_head[:, None] == lane_head[None, :], 1.0 / HEAD_DIM, 0.0).astype(BF16)

    w_in, w_out, pool_w_bf = mix_w_in.astype(BF16), mix_w_out.astype(BF16), pool_w.astype(BF16)
    w_up, w_down = ffn_w_up.astype(BF16), ffn_w_down.astype(BF16)
    g_all = norm_g.reshape(depth * norms_per_layer, 1, d)
    pool_scale3 = pool_scale[:, None, :]

    x2d = x.reshape(batch * seq, d)
    for i in range(depth):
        g_row = i * norms_per_layer
        if i % 2 == 0:
            e = i // 2
            qg = jnp.tile(q_norm_g[e], N_Q_HEADS)[None, :]
            kg = jnp.tile(k_norm_g[e], N_KV_HEADS)[None, :]
            qt, ka, kb, vt, conv = _inproj(
                x2d, g_all, g_row, w_in, e, qg, kg, bd, c_tab, su_tab, sd_tab, sconv_w, seq=seq)
            attn = _attention(qt, ka, kb, vt, batch=batch, seq=seq)
            x2d = _tail_even(attn, conv, x2d, w_out, e, g_all, g_row, w_up, ffn_conv_w, w_down, i,
                             seq=seq)
        else:
            x2d = _tail_odd(x2d, pool_w_bf, pool_scale3, i // 2, g_all, g_row, w_up, ffn_conv_w,
                            w_down, i, seq=seq)
    return x2d.reshape(batch, seq, d)
```

```python
import functools
import math

import jax
import jax.numpy as jnp
from jax import lax
from jax.experimental import pallas as pl
from jax.experimental.pallas import tpu as pltpu

F32 = jnp.float32
BF16 = jnp.bfloat16

GRID_W = 64
HEAD_DIM = 64
N_Q_HEADS = 8
N_KV_HEADS = 2
Q_GROUP = N_Q_HEADS // N_KV_HEADS
ATTN_DIM = N_Q_HEADS * HEAD_DIM
KV_DIM = N_KV_HEADS * HEAD_DIM
CONV_DIM = 512
ROPE_THETA = 10000.0
ROPE_AXIS_DIM = HEAD_DIM // 2
POOL_WINDOWS = (2, 4, 8, 16)
RMS_EPS = 1e-6

LANES = 128
SUBLANES = 8
BF16_ROWS = 16
HALO = SUBLANES
MXU_DIM = 256

VMEM_LIMIT = 56 * 1024 * 1024

TM_PROJ = 512
TM_FFN = 512
TQ_ATTN = 512
ATTN_KEY_BLOCK = 256
ATTN_Q_BLOCK = 256
ATTN_LOOKAHEAD = 6
FFN_CHUNK = 256


def _rmsnorm(x, g):
    ms = jnp.mean(x * x, axis=-1, keepdims=True)
    return x * lax.rsqrt(ms + RMS_EPS) * g


def _dot(a, b):
    return jnp.dot(a, b, preferred_element_type=F32)


def _const_spec(shape):
    zeros = (0,) * len(shape)
    return pl.BlockSpec(shape, lambda *_: zeros, pipeline_mode=pl.Buffered(1))


def _layer_spec(arr, layer, block=None, pos=None):
    block = tuple(arr.shape[1:]) if block is None else tuple(block)
    pos = (0,) * len(block) if pos is None else tuple(pos)
    index = (layer,) + pos
    return pl.BlockSpec((None,) + block, lambda *_: index, pipeline_mode=pl.Buffered(1))


def _halo_specs(tm, n_rows, d):
    per = tm // HALO
    last = n_rows // HALO - 1
    main = pl.BlockSpec((tm, d), lambda i: (i, 0))
    prev = pl.BlockSpec((HALO, d), lambda i: (jnp.maximum(i * per - 1, 0), 0))
    nxt = pl.BlockSpec((HALO, d), lambda i: (jnp.minimum((i + 1) * per, last), 0))
    return main, prev, nxt


def _load_ext(x_ref, xp_ref, xn_ref, tiles_per_seq):
    pos = pl.program_id(0) % tiles_per_seq
    xp = jnp.where(pos != 0, xp_ref[...], 0.0)
    xn = jnp.where(pos != tiles_per_seq - 1, xn_ref[...], 0.0)
    return jnp.concatenate([x_ref[...], xp, xn], axis=0)


def _shift_rows(pm, prev_row, next_row):
    tm = pm.shape[0]
    r = lax.broadcasted_iota(jnp.int32, (SUBLANES, pm.shape[1]), 0)
    m1 = pltpu.roll(pm, 1, 0)
    p1 = pltpu.roll(pm, tm - 1, 0)
    m1 = jnp.concatenate(
        [jnp.where(r == 0, prev_row, m1[:SUBLANES]), m1[SUBLANES:]], axis=0)
    p1 = jnp.concatenate(
        [p1[:tm - SUBLANES], jnp.where(r == SUBLANES - 1, next_row, p1[tm - SUBLANES:])], axis=0)
    return m1, p1


def _dwconv3_ext(p, cw):
    tm = p.shape[0] - 2 * HALO
    pm = p[:tm]
    m1, p1 = _shift_rows(pm, p[tm + HALO - 1:tm + HALO], p[tm + HALO:tm + HALO + 1])
    return m1 * cw[0:1] + pm * cw[1:2] + p1 * cw[2:3]


def _head_mean_square(v, bd):
    sq = v * v
    hi = sq.astype(BF16)
    lo = (sq - hi.astype(F32)).astype(BF16)
    outs = []
    for c0 in range(0, v.shape[1], MXU_DIM):
        wd = min(MXU_DIM, v.shape[1] - c0)
        m = bd[:wd, :wd]
        outs.append(_dot(hi[:, c0:c0 + wd], m) + _dot(lo[:, c0:c0 + wd], m))
    return outs[0] if len(outs) == 1 else jnp.concatenate(outs, axis=1)


def _rope(v, c, su, sd):
    outs = []
    for s in range(v.shape[1] // LANES):
        vs = v[:, s * LANES:(s + 1) * LANES]
        up = pltpu.roll(vs, ROPE_AXIS_DIM // 2, 1)
        dn = pltpu.roll(vs, LANES - ROPE_AXIS_DIM // 2, 1)
        outs.append(vs * c + up * su + dn * sd)
    return outs[0] if len(outs) == 1 else jnp.concatenate(outs, axis=1)


def _inproj_kernel(x_ref, xp_ref, xn_ref, g_ref, w_ref, qg_ref, kg_ref, bd_ref,
                   c_ref, su_ref, sd_ref, cw_ref,
                   qt_out, ka_out, kb_out, vt_out, conv_out,
                   *, tm, tiles_per_seq, q_scale):
    x_ext = _load_ext(x_ref, xp_ref, xn_ref, tiles_per_seq)
    h_ext = _rmsnorm(x_ext, g_ref[...]).astype(BF16)
    n_head_cols = ATTN_DIM + 2 * KV_DIM + CONV_DIM
    main = _dot(h_ext[:tm], w_ref[:, :n_head_cols])

    bd = bd_ref[...]
    c, su, sd = c_ref[...], su_ref[...], sd_ref[...]

    q_raw = main[:, :ATTN_DIM]
    qn = q_raw * lax.rsqrt(_head_mean_square(q_raw, bd) + RMS_EPS) * qg_ref[...]
    qt_out[...] = (_rope(qn, c, su, sd) * q_scale).T.astype(qt_out.dtype)
    vt_out[...] = main[:, ATTN_DIM + KV_DIM:ATTN_DIM + 2 * KV_DIM].T.astype(vt_out.dtype)

    k_raw = main[:, ATTN_DIM:ATTN_DIM + KV_DIM]
    kn = k_raw * lax.rsqrt(_head_mean_square(k_raw, bd) + RMS_EPS) * kg_ref[...]
    kr = _rope(kn, c, su, sd)
    sw = pltpu.roll(kr, HEAD_DIM, 1)
    low = lax.broadcasted_iota(jnp.int32, (tm, LANES), 1) < HEAD_DIM
    ka_out[0] = jnp.where(low, kr, 0.0).astype(ka_out.dtype)
    kb_out[0] = jnp.where(low, 0.0, sw).astype(kb_out.dtype)
    ka_out[1] = jnp.where(low, sw, 0.0).astype(ka_out.dtype)
    kb_out[1] = jnp.where(low, 0.0, kr).astype(kb_out.dtype)

    gate_b = main[:, ATTN_DIM + 2 * KV_DIM:]
    for c0 in range(0, CONV_DIM, MXU_DIM):
        gate_c = _dot(h_ext, w_ref[:, n_head_cols + c0:n_head_cols + c0 + MXU_DIM])
        conv_in = _dot(h_ext, w_ref[:, n_head_cols + CONV_DIM + c0:n_head_cols + CONV_DIM + c0 + MXU_DIM])
        conv = _dwconv3_ext(gate_c * conv_in, cw_ref[:, c0:c0 + MXU_DIM])
        conv_out[:, c0:c0 + MXU_DIM] = (gate_b[:, c0:c0 + MXU_DIM] * conv).astype(conv_out.dtype)


def _inproj(x2d, g_all, g_row, w_in, layer, qg, kg, bd, c_tab, su_tab, sd_tab, sconv_w, *, seq):
    n_rows, d = x2d.shape
    tm = TM_PROJ
    tiles_per_seq = seq // tm
    main, prev, nxt = _halo_specs(tm, n_rows, d)
    tab = pl.BlockSpec((tm, LANES), lambda i: (i % tiles_per_seq, 0))
    k_spec = pl.BlockSpec((N_KV_HEADS, tm, LANES), lambda i: (0, i, 0))
    k_shape = jax.ShapeDtypeStruct((N_KV_HEADS, n_rows, LANES), BF16)
    q_scale = HEAD_DIM ** -0.5 * math.log2(math.e)
    return pl.pallas_call(
        functools.partial(_inproj_kernel, tm=tm, tiles_per_seq=tiles_per_seq, q_scale=q_scale),
        grid=(n_rows // tm,),
        in_specs=[main, prev, nxt, _layer_spec(g_all, g_row), _layer_spec(w_in, layer),
                  _const_spec(qg.shape), _const_spec(kg.shape), _const_spec(bd.shape),
                  tab, tab, tab, _layer_spec(sconv_w, layer)],
        out_specs=[pl.BlockSpec((ATTN_DIM, tm), lambda i: (0, i)),
                   k_spec, k_spec,
                   pl.BlockSpec((KV_DIM, tm), lambda i: (0, i)),
                   pl.BlockSpec((tm, CONV_DIM), lambda i: (i, 0))],
        out_shape=[jax.ShapeDtypeStruct((ATTN_DIM, n_rows), BF16),
                   k_shape, k_shape,
                   jax.ShapeDtypeStruct((KV_DIM, n_rows), BF16),
                   jax.ShapeDtypeStruct((n_rows, CONV_DIM), BF16)],
        compiler_params=pltpu.CompilerParams(
            dimension_semantics=("parallel",), vmem_limit_bytes=VMEM_LIMIT),
        name="inproj",
    )(x2d, x2d, x2d, g_all, w_in, qg, kg, bd, c_tab, su_tab, sd_tab, sconv_w)


def _attn_kernel(qt_ref, ka_ref, kb_ref, vt_ref, o_ref):
    seq = vt_ref.shape[1]
    tq = qt_ref.shape[1]
    k_refs = (ka_ref, kb_ref, ka_ref, kb_ref)
    q_cols = [slice(c0, c0 + ATTN_Q_BLOCK) for c0 in range(0, tq, ATTN_Q_BLOCK)]
    units = [(g, k0, qi) for g in range(Q_GROUP) for k0 in range(0, seq, ATTN_KEY_BLOCK)
             for qi in range(len(q_cols))]

    def scores(u):
        g, k0, qi = units[u]
        slab = g // 2
        return _dot(k_refs[g][k0:k0 + ATTN_KEY_BLOCK, :],
                    qt_ref[slab * LANES:(slab + 1) * LANES, q_cols[qi]])

    pending = [scores(u) for u in range(min(ATTN_LOOKAHEAD, len(units)))]
    ones = jnp.ones((BF16_ROWS, ATTN_KEY_BLOCK), BF16)
    outs = [[None] * len(q_cols) for _ in range(Q_GROUP)]
    m = [None] * len(q_cols)
    acc = [None] * len(q_cols)
    for u, (g, k0, qi) in enumerate(units):
        st = pending.pop(0)
        if u + ATTN_LOOKAHEAD < len(units):
            pending.append(scores(u + ATTN_LOOKAHEAD))
        vt1 = jnp.concatenate([vt_ref[:, k0:k0 + ATTN_KEY_BLOCK], ones], axis=0)
        blk_max = jnp.max(st, axis=0, keepdims=True)
        if k0 == 0:
            m[qi] = blk_max
            acc[qi] = _dot(vt1, jnp.exp2(st - blk_max).astype(BF16))
        else:
            m_new = jnp.maximum(m[qi], blk_max)
            acc[qi] = (jnp.exp2(m[qi] - m_new) * acc[qi]
                       + _dot(vt1, jnp.exp2(st - m_new).astype(BF16)))
            m[qi] = m_new
        if k0 + ATTN_KEY_BLOCK == seq:
            outs[g][qi] = acc[qi][:HEAD_DIM] * (1.0 / acc[qi][HEAD_DIM:HEAD_DIM + 1])
    o_t = jnp.concatenate([jnp.concatenate(row, axis=1) for row in outs], axis=0)
    o_ref[...] = o_t.T.astype(o_ref.dtype)


def _attention(qt, ka, kb, vt, *, batch, seq):
    tq = TQ_ATTN
    q_tiles = seq // tq
    group_cols = Q_GROUP * HEAD_DIM
    k_spec = pl.BlockSpec((None, seq, LANES), lambda b, h, i: (h, b, 0))
    return pl.pallas_call(
        _attn_kernel,
        grid=(batch, N_KV_HEADS, q_tiles),
        in_specs=[pl.BlockSpec((group_cols, tq), lambda b, h, i: (h, b * q_tiles + i)),
                  k_spec, k_spec,
                  pl.BlockSpec((HEAD_DIM, seq), lambda b, h, i: (h, b))],
        out_specs=pl.BlockSpec((tq, group_cols), lambda b, h, i: (b * q_tiles + i, h)),
        out_shape=jax.ShapeDtypeStruct((qt.shape[1], qt.shape[0]), BF16),
        compiler_params=pltpu.CompilerParams(
            dimension_semantics=("parallel", "parallel", "arbitrary"),
            vmem_limit_bytes=VMEM_LIMIT),
        name="attention",
    )(qt, ka, kb, vt)


def _halo16_specs(tm, n_rows, cols):
    per = tm // BF16_ROWS
    last = n_rows // BF16_ROWS - 1
    prev = pl.BlockSpec((BF16_ROWS, cols), lambda i: (jnp.maximum(i * per - 1, 0), 0))
    nxt = pl.BlockSpec((BF16_ROWS, cols), lambda i: (jnp.minimum((i + 1) * per, last), 0))
    return prev, nxt


def _ffn_core(x1_ext, g2, wup_ref, cw_ref, wdn_ref, g3, o_ref, act_ref, x1_ref,
              *, tiles_per_seq, chunk):
    tm, d_ff = act_ref.shape
    pos = pl.program_id(0) % tiles_per_seq
    x1_ref[...] = x1_ext[:tm]
    x1p = jnp.where(pos != 0, x1_ext[tm:tm + HALO], 0.0)
    x1n = jnp.where(pos != tiles_per_seq - 1, x1_ext[tm + HALO:], 0.0)
    h_ext = _rmsnorm(jnp.concatenate([x1_ref[...], x1p, x1n], axis=0), g2).astype(BF16)

    def up_branch(c0):
        p = _dot(h_ext, wup_ref[:, c0:c0 + chunk])
        return _dwconv3_ext(p, cw_ref[:, c0:c0 + chunk])

    for c0 in range(0, d_ff, chunk):
        u = up_branch(c0)
        gt = up_branch(d_ff + c0)
        act = jax.nn.gelu(gt, approximate=True) * u
        act_ref[:, c0:c0 + chunk] = act.astype(act_ref.dtype)
    half = tm // 2
    for r0 in (0, half):
        y = _dot(act_ref[r0:r0 + half], wdn_ref[...])
        o_ref[r0:r0 + half] = x1_ref[r0:r0 + half] + _rmsnorm(y, g3)


def _tail_even_kernel(a_ref, ap_ref, an_ref, c_ref, cp_ref, cn_ref, x_ref, xp_ref, xn_ref,
                      wa_ref, wc_ref, g1_ref, g2_ref, wup_ref, cw_ref, wdn_ref, g3_ref,
                      o_ref, act_ref, x1_ref, *, tiles_per_seq, chunk):
    def ext(m_ref, p_ref, n_ref):
        halo = jnp.concatenate([p_ref[...].astype(F32)[HALO:], n_ref[...].astype(F32)[:HALO]], axis=0)
        return jnp.concatenate([m_ref[...], halo.astype(m_ref.dtype)], axis=0)

    mix = _dot(ext(a_ref, ap_ref, an_ref), wa_ref[...]) + _dot(ext(c_ref, cp_ref, cn_ref), wc_ref[...])
    x_ext = jnp.concatenate([x_ref[...], xp_ref[HALO:], xn_ref[:HALO]], axis=0)
    x1_ext = x_ext + _rmsnorm(mix, g1_ref[...])
    _ffn_core(x1_ext, g2_ref[...], wup_ref, cw_ref, wdn_ref, g3_ref[...], o_ref, act_ref, x1_ref,
              tiles_per_seq=tiles_per_seq, chunk=chunk)


def _tail_odd_kernel(x_ref, xp_ref, xn_ref, g0_ref, pw_ref, ps_ref,
                     g1_ref, g2_ref, wup_ref, cw_ref, wdn_ref, g3_ref,
                     o_ref, act_ref, x1_ref, *, tiles_per_seq, chunk, seq):
    tm, d = x_ref.shape
    pos = pl.program_id(0) % tiles_per_seq
    xp = jnp.where(pos != 0, xp_ref[...], 0.0)
    xn = jnp.where(pos != tiles_per_seq - 1, xn_ref[...], 0.0)
    x_wide = jnp.concatenate([xp, x_ref[...], xn], axis=0)
    h_wide = _rmsnorm(x_wide, g0_ref[...])
    n_wide = tm + 2 * BF16_ROWS
    n_ext = tm + 2 * HALO
    group = d // len(POOL_WINDOWS)

    def reorder(a):
        return jnp.concatenate([a[HALO:HALO + tm], a[:HALO], a[HALO + tm:]], axis=0)

    t = pos * tm - HALO + lax.broadcasted_iota(jnp.int32, (n_ext, LANES), 0)
    outs = []
    for gi, w in enumerate(POOL_WINDOWS):
        hg = h_wide[:, gi * group:(gi + 1) * group]
        acc, span = hg, 1
        while span < w:
            acc = acc + pltpu.roll(acc, span, 0)
            span *= 2
        ahead = w // 2 - 1
        if ahead:
            acc = pltpu.roll(acc, n_wide - ahead, 0)
        win = acc[HALO:HALO + n_ext]
        cnt = jnp.minimum(t + w // 2, seq) - jnp.maximum(t - w // 2, 0)
        inv = 1.0 / jnp.maximum(cnt, 1).astype(F32)
        inv = jnp.concatenate([inv] * (group // LANES), axis=1)
        pooled = win * inv - hg[HALO:HALO + n_ext]
        outs.append(_dot(reorder(pooled).astype(BF16), pw_ref[gi]))
    mix = jnp.concatenate(outs, axis=1) * ps_ref[...]
    x1_ext = reorder(x_wide[HALO:HALO + n_ext]) + _rmsnorm(mix, g1_ref[...])
    _ffn_core(x1_ext, g2_ref[...], wup_ref, cw_ref, wdn_ref, g3_ref[...], o_ref, act_ref, x1_ref,
              tiles_per_seq=tiles_per_seq, chunk=chunk)


def _tail_call(body, row_inputs, row_specs, const_inputs, const_specs, x2d, d_ff, name):
    n_rows, d = x2d.shape
    tm = TM_FFN
    return pl.pallas_call(
        body,
        grid=(n_rows // tm,),
        in_specs=list(row_specs) + list(const_specs),
        out_specs=pl.BlockSpec((tm, d), lambda i: (i, 0)),
        out_shape=jax.ShapeDtypeStruct(x2d.shape, x2d.dtype),
        scratch_shapes=[pltpu.VMEM((tm, d_ff), BF16), pltpu.VMEM((tm, d), F32)],
        compiler_params=pltpu.CompilerParams(
            dimension_semantics=("parallel",), vmem_limit_bytes=VMEM_LIMIT),
        name=name,
    )(*row_inputs, *const_inputs)


def _tail_even(attn, conv, x2d, w_out, e, g_all, g_row, w_up, conv_w, w_down, layer, *, seq):
    n_rows, d = x2d.shape
    tm = TM_FFN
    row = lambda cols: pl.BlockSpec((tm, cols), lambda i: (i, 0))
    a_prev, a_next = _halo16_specs(tm, n_rows, ATTN_DIM)
    c_prev, c_next = _halo16_specs(tm, n_rows, CONV_DIM)
    x_prev, x_next = _halo16_specs(tm, n_rows, d)
    body = functools.partial(_tail_even_kernel, tiles_per_seq=seq // tm, chunk=FFN_CHUNK)
    return _tail_call(
        body,
        (attn, attn, attn, conv, conv, conv, x2d, x2d, x2d),
        (row(ATTN_DIM), a_prev, a_next, row(CONV_DIM), c_prev, c_next, row(d), x_prev, x_next),
        (w_out, w_out, g_all, g_all, w_up, conv_w, w_down, g_all),
        (_layer_spec(w_out, e, (ATTN_DIM, d), (0, 0)),
         _layer_spec(w_out, e, (CONV_DIM, d), (ATTN_DIM // CONV_DIM, 0)),
         _layer_spec(g_all, g_row + 1), _layer_spec(g_all, g_row + 2),
         _layer_spec(w_up, layer), _layer_spec(conv_w, layer), _layer_spec(w_down, layer),
         _layer_spec(g_all, g_row + 3)),
        x2d, w_down.shape[1], "tail_even")


def _tail_odd(x2d, pool_w, pool_scale, o, g_all, g_row, w_up, conv_w, w_down, layer, *, seq):
    n_rows, d = x2d.shape
    tm = TM_FFN
    x_prev, x_next = _halo16_specs(tm, n_rows, d)
    body = functools.partial(_tail_odd_kernel, tiles_per_seq=seq // tm, chunk=FFN_CHUNK, seq=seq)
    return _tail_call(
        body,
        (x2d, x2d, x2d),
        (pl.BlockSpec((tm, d), lambda i: (i, 0)), x_prev, x_next),
        (g_all, pool_w, pool_scale, g_all, g_all, w_up, conv_w, w_down, g_all),
        (_layer_spec(g_all, g_row), _layer_spec(pool_w, o), _layer_spec(pool_scale, o),
         _layer_spec(g_all, g_row + 1), _layer_spec(g_all, g_row + 2),
         _layer_spec(w_up, layer), _layer_spec(conv_w, layer), _layer_spec(w_down, layer),
         _layer_spec(g_all, g_row + 3)),
        x2d, w_down.shape[1], "tail_odd")


def _rope_tables(seq):
    rows = seq // GRID_W
    row = jnp.repeat(jnp.arange(rows, dtype=F32), GRID_W)
    col = jnp.tile(jnp.arange(GRID_W, dtype=F32), rows)
    inv = ROPE_THETA ** (-jnp.arange(0, ROPE_AXIS_DIM, 2, dtype=F32) / ROPE_AXIS_DIM)
    ang_r = row[:, None] * inv[None, :]
    ang_c = col[:, None] * inv[None, :]
    cos_r, sin_r, cos_c, sin_c = jnp.cos(ang_r), jnp.sin(ang_r), jnp.cos(ang_c), jnp.sin(ang_c)
    zero = jnp.zeros_like(sin_r)
    c = jnp.concatenate([cos_r, cos_r, cos_c, cos_c], axis=1)
    su = jnp.concatenate([zero, sin_r, zero, sin_c], axis=1)
    sd = jnp.concatenate([-sin_r, zero, -sin_c, zero], axis=1)
    reps = LANES // HEAD_DIM
    return tuple(jnp.tile(t, (1, reps)) for t in (c, su, sd))


def kernel(x, norm_g, mix_w_in, q_norm_g, k_norm_g, sconv_w, mix_w_out, pool_w, pool_scale,
           ffn_w_up, ffn_conv_w, ffn_w_down):
    batch, seq, d = x.shape
    depth, norms_per_layer = norm_g.shape[:2]
    assert seq % TM_PROJ == 0 and seq % TM_FFN == 0 and seq % TQ_ATTN == 0
    assert ffn_w_down.shape[1] % FFN_CHUNK == 0

    c_tab, su_tab, sd_tab = _rope_tables(seq)
    lane_head = jnp.arange(MXU_DIM) // HEAD_DIM
    bd = jnp.where(lane_head[:, None] == lane_head[None, :], 1.0 / HEAD_DIM, 0.0).astype(BF16)

    w_in, w_out, pool_w_bf = mix_w_in.astype(BF16), mix_w_out.astype(BF16), pool_w.astype(BF16)
    w_up, w_down = ffn_w_up.astype(BF16), ffn_w_down.astype(BF16)
    g_all = norm_g.reshape(depth * norms_per_layer, 1, d)
    pool_scale3 = pool_scale[:, None, :]

    x2d = x.reshape(batch * seq, d)
    for i in range(depth):
        g_row = i * norms_per_layer
        if i % 2 == 0:
            e = i // 2
            qg = jnp.tile(q_norm_g[e], N_Q_HEADS)[None, :]
            kg = jnp.tile(k_norm_g[e], N_KV_HEADS)[None, :]
            qt, ka, kb, vt, conv = _inproj(
                x2d, g_all, g_row, w_in, e, qg, kg, bd, c_tab, su_tab, sd_tab, sconv_w, seq=seq)
            attn = _attention(qt, ka, kb, vt, batch=batch, seq=seq)
            x2d = _tail_even(attn, conv, x2d, w_out, e, g_all, g_row, w_up, ffn_conv_w, w_down, i,
                             seq=seq)
        else:
            x2d = _tail_odd(x2d, pool_w_bf, pool_scale3, i // 2, g_all, g_row, w_up, ffn_conv_w,
                            w_down, i, seq=seq)
    return x2d.reshape(batch, seq, d)
```

```python
import functools
import math

import jax
import jax.numpy as jnp
from jax import lax
from jax.experimental import pallas as pl
from jax.experimental.pallas import tpu as pltpu

F32 = jnp.float32
BF16 = jnp.bfloat16

GRID_W = 64
HEAD_DIM = 64
N_Q_HEADS = 8
N_KV_HEADS = 2
Q_GROUP = N_Q_HEADS // N_KV_HEADS
ATTN_DIM = N_Q_HEADS * HEAD_DIM
KV_DIM = N_KV_HEADS * HEAD_DIM
CONV_DIM = 512
ROPE_THETA = 10000.0
ROPE_AXIS_DIM = HEAD_DIM // 2
POOL_WINDOWS = (2, 4, 8, 16)
RMS_EPS = 1e-6

LANES = 128
SUBLANES = 8
BF16_ROWS = 16
HALO = SUBLANES
MXU_DIM = 256

VMEM_LIMIT = 56 * 1024 * 1024

TM_PROJ = 512
TM_FFN = 512
TQ_ATTN = 1024
ATTN_KEY_BLOCK = 256
ATTN_Q_BLOCK = 256
ATTN_LOOKAHEAD = 6
FFN_CHUNK = 256


def _rmsnorm(x, g):
    ms = jnp.mean(x * x, axis=-1, keepdims=True)
    return x * lax.rsqrt(ms + RMS_EPS) * g


def _dot(a, b):
    return jnp.dot(a, b, preferred_element_type=F32)


def _const_spec(shape):
    zeros = (0,) * len(shape)
    return pl.BlockSpec(shape, lambda *_: zeros, pipeline_mode=pl.Buffered(1))


def _layer_spec(arr, layer, block=None, pos=None):
    block = tuple(arr.shape[1:]) if block is None else tuple(block)
    pos = (0,) * len(block) if pos is None else tuple(pos)
    index = (layer,) + pos
    return pl.BlockSpec((None,) + block, lambda *_: index, pipeline_mode=pl.Buffered(1))


def _halo_specs(tm, n_rows, d):
    per = tm // HALO
    last = n_rows // HALO - 1
    main = pl.BlockSpec((tm, d), lambda i: (i, 0))
    prev = pl.BlockSpec((HALO, d), lambda i: (jnp.maximum(i * per - 1, 0), 0))
    nxt = pl.BlockSpec((HALO, d), lambda i: (jnp.minimum((i + 1) * per, last), 0))
    return main, prev, nxt


def _load_ext(x_ref, xp_ref, xn_ref, tiles_per_seq):
    pos = pl.program_id(0) % tiles_per_seq
    xp = jnp.where(pos != 0, xp_ref[...], 0.0)
    xn = jnp.where(pos != tiles_per_seq - 1, xn_ref[...], 0.0)
    return jnp.concatenate([x_ref[...], xp, xn], axis=0)


def _shift_rows(pm, prev_row, next_row):
    tm = pm.shape[0]
    r = lax.broadcasted_iota(jnp.int32, (SUBLANES, pm.shape[1]), 0)
    m1 = pltpu.roll(pm, 1, 0)
    p1 = pltpu.roll(pm, tm - 1, 0)
    m1 = jnp.concatenate(
        [jnp.where(r == 0, prev_row, m1[:SUBLANES]), m1[SUBLANES:]], axis=0)
    p1 = jnp.concatenate(
        [p1[:tm - SUBLANES], jnp.where(r == SUBLANES - 1, next_row, p1[tm - SUBLANES:])], axis=0)
    return m1, p1


def _dwconv3_ext(p, cw):
    tm = p.shape[0] - 2 * HALO
    pm = p[:tm]
    m1, p1 = _shift_rows(pm, p[tm + HALO - 1:tm + HALO], p[tm + HALO:tm + HALO + 1])
    return m1 * cw[0:1] + pm * cw[1:2] + p1 * cw[2:3]


def _head_mean_square(v, bd):
    sq = v * v
    hi = sq.astype(BF16)
    lo = (sq - hi.astype(F32)).astype(BF16)
    outs = []
    for c0 in range(0, v.shape[1], MXU_DIM):
        wd = min(MXU_DIM, v.shape[1] - c0)
        m = bd[:wd, :wd]
        outs.append(_dot(hi[:, c0:c0 + wd], m) + _dot(lo[:, c0:c0 + wd], m))
    return outs[0] if len(outs) == 1 else jnp.concatenate(outs, axis=1)


def _rope(v, c, su, sd):
    outs = []
    for s in range(v.shape[1] // LANES):
        vs = v[:, s * LANES:(s + 1) * LANES]
        up = pltpu.roll(vs, ROPE_AXIS_DIM // 2, 1)
        dn = pltpu.roll(vs, LANES - ROPE_AXIS_DIM // 2, 1)
        outs.append(vs * c + up * su + dn * sd)
    return outs[0] if len(outs) == 1 else jnp.concatenate(outs, axis=1)


def _inproj_kernel(x_ref, xp_ref, xn_ref, g_ref, w_ref, qg_ref, kg_ref, bd_ref,
                   c_ref, su_ref, sd_ref, cw_ref,
                   qt_out, ka_out, kb_out, vt_out, conv_out,
                   *, tm, tiles_per_seq, q_scale):
    x_ext = _load_ext(x_ref, xp_ref, xn_ref, tiles_per_seq)
    h_ext = _rmsnorm(x_ext, g_ref[...]).astype(BF16)
    n_head_cols = ATTN_DIM + 2 * KV_DIM + CONV_DIM
    main = _dot(h_ext[:tm], w_ref[:, :n_head_cols])

    bd = bd_ref[...]
    c, su, sd = c_ref[...], su_ref[...], sd_ref[...]

    q_raw = main[:, :ATTN_DIM]
    qn = q_raw * lax.rsqrt(_head_mean_square(q_raw, bd) + RMS_EPS) * qg_ref[...]
    qt_out[...] = (_rope(qn, c, su, sd) * q_scale).T.astype(qt_out.dtype)
    vt_out[...] = main[:, ATTN_DIM + KV_DIM:ATTN_DIM + 2 * KV_DIM].T.astype(vt_out.dtype)

    k_raw = main[:, ATTN_DIM:ATTN_DIM + KV_DIM]
    kn = k_raw * lax.rsqrt(_head_mean_square(k_raw, bd) + RMS_EPS) * kg_ref[...]
    kr = _rope(kn, c, su, sd)
    sw = pltpu.roll(kr, HEAD_DIM, 1)
    low = lax.broadcasted_iota(jnp.int32, (tm, LANES), 1) < HEAD_DIM
    ka_out[0] = jnp.where(low, kr, 0.0).astype(ka_out.dtype)
    kb_out[0] = jnp.where(low, 0.0, sw).astype(kb_out.dtype)
    ka_out[1] = jnp.where(low, sw, 0.0).astype(ka_out.dtype)
    kb_out[1] = jnp.where(low, 0.0, kr).astype(kb_out.dtype)

    gate_b = main[:, ATTN_DIM + 2 * KV_DIM:]
    for c0 in range(0, CONV_DIM, MXU_DIM):
        gate_c = _dot(h_ext, w_ref[:, n_head_cols + c0:n_head_cols + c0 + MXU_DIM])
        conv_in = _dot(h_ext, w_ref[:, n_head_cols + CONV_DIM + c0:n_head_cols + CONV_DIM + c0 + MXU_DIM])
        conv = _dwconv3_ext(gate_c * conv_in, cw_ref[:, c0:c0 + MXU_DIM])
        conv_out[:, c0:c0 + MXU_DIM] = (gate_b[:, c0:c0 + MXU_DIM] * conv).astype(conv_out.dtype)


def _inproj(x2d, g_all, g_row, w_in, layer, qg, kg, bd, c_tab, su_tab, sd_tab, sconv_w, *, seq):
    n_rows, d = x2d.shape
    tm = TM_PROJ
    tiles_per_seq = seq // tm
    main, prev, nxt = _halo_specs(tm, n_rows, d)
    tab = pl.BlockSpec((tm, LANES), lambda i: (i % tiles_per_seq, 0))
    k_spec = pl.BlockSpec((N_KV_HEADS, tm, LANES), lambda i: (0, i, 0))
    k_shape = jax.ShapeDtypeStruct((N_KV_HEADS, n_rows, LANES), BF16)
    q_scale = HEAD_DIM ** -0.5 * math.log2(math.e)
    return pl.pallas_call(
        functools.partial(_inproj_kernel, tm=tm, tiles_per_seq=tiles_per_seq, q_scale=q_scale),
        grid=(n_rows // tm,),
        in_specs=[main, prev, nxt, _layer_spec(g_all, g_row), _layer_spec(w_in, layer),
                  _const_spec(qg.shape), _const_spec(kg.shape), _const_spec(bd.shape),
                  tab, tab, tab, _layer_spec(sconv_w, layer)],
        out_specs=[pl.BlockSpec((ATTN_DIM, tm), lambda i: (0, i)),
                   k_spec, k_spec,
                   pl.BlockSpec((KV_DIM, tm), lambda i: (0, i)),
                   pl.BlockSpec((tm, CONV_DIM), lambda i: (i, 0))],
        out_shape=[jax.ShapeDtypeStruct((ATTN_DIM, n_rows), BF16),
                   k_shape, k_shape,
                   jax.ShapeDtypeStruct((KV_DIM, n_rows), BF16),
                   jax.ShapeDtypeStruct((n_rows, CONV_DIM), BF16)],
        compiler_params=pltpu.CompilerParams(
            dimension_semantics=("parallel",), vmem_limit_bytes=VMEM_LIMIT),
        name="inproj",
    )(x2d, x2d, x2d, g_all, w_in, qg, kg, bd, c_tab, su_tab, sd_tab, sconv_w)


def _attn_kernel(qt_ref, ka_ref, kb_ref, vt_ref, o_ref):
    seq = vt_ref.shape[1]
    tq = qt_ref.shape[1]
    k_refs = (ka_ref, kb_ref, ka_ref, kb_ref)
    q_cols = [slice(c0, c0 + ATTN_Q_BLOCK) for c0 in range(0, tq, ATTN_Q_BLOCK)]
    units = [(g, k0, qi) for g in range(Q_GROUP) for k0 in range(0, seq, ATTN_KEY_BLOCK)
             for qi in range(len(q_cols))]

    def scores(u):
        g, k0, qi = units[u]
        slab = g // 2
        return _dot(k_refs[g][k0:k0 + ATTN_KEY_BLOCK, :],
                    qt_ref[slab * LANES:(slab + 1) * LANES, q_cols[qi]])

    pending = [scores(u) for u in range(min(ATTN_LOOKAHEAD, len(units)))]
    ones = jnp.ones((BF16_ROWS, ATTN_KEY_BLOCK), BF16)
    outs = [[None] * len(q_cols) for _ in range(Q_GROUP)]
    m = [None] * len(q_cols)
    acc = [None] * len(q_cols)
    for u, (g, k0, qi) in enumerate(units):
        st = pending.pop(0)
        if u + ATTN_LOOKAHEAD < len(units):
            pending.append(scores(u + ATTN_LOOKAHEAD))
        vt1 = jnp.concatenate([vt_ref[:, k0:k0 + ATTN_KEY_BLOCK], ones], axis=0)
        blk_max = jnp.max(st, axis=0, keepdims=True)
        if k0 == 0:
            m[qi] = blk_max
            acc[qi] = _dot(vt1, jnp.exp2(st - blk_max).astype(BF16))
        else:
            m_new = jnp.maximum(m[qi], blk_max)
            acc[qi] = (jnp.exp2(m[qi] - m_new) * acc[qi]
                       + _dot(vt1, jnp.exp2(st - m_new).astype(BF16)))
            m[qi] = m_new
        if k0 + ATTN_KEY_BLOCK == seq:
            outs[g][qi] = acc[qi][:HEAD_DIM] * (1.0 / acc[qi][HEAD_DIM:HEAD_DIM + 1])
    o_t = jnp.concatenate([jnp.concatenate(row, axis=1) for row in outs], axis=0)
    o_ref[...] = o_t.T.astype(o_ref.dtype)


def _attention(qt, ka, kb, vt, *, batch, seq):
    tq = TQ_ATTN
    q_tiles = seq // tq
    group_cols = Q_GROUP * HEAD_DIM
    k_spec = pl.BlockSpec((None, seq, LANES), lambda b, h, i: (h, b, 0))
    return pl.pallas_call(
        _attn_kernel,
        grid=(batch, N_KV_HEADS, q_tiles),
        in_specs=[pl.BlockSpec((group_cols, tq), lambda b, h, i: (h, b * q_tiles + i)),
                  k_spec, k_spec,
                  pl.BlockSpec((HEAD_DIM, seq), lambda b, h, i: (h, b))],
        out_specs=pl.BlockSpec((tq, group_cols), lambda b, h, i: (b * q_tiles + i, h)),
        out_shape=jax.ShapeDtypeStruct((qt.shape[1], qt.shape[0]), BF16),
        compiler_params=pltpu.CompilerParams(
            dimension_semantics=("parallel", "parallel", "arbitrary"),
            vmem_limit_bytes=VMEM_LIMIT),
        name="attention",
    )(qt, ka, kb, vt)


def _halo16_specs(tm, n_rows, cols):
    per = tm // BF16_ROWS
    last = n_rows // BF16_ROWS - 1
    prev = pl.BlockSpec((BF16_ROWS, cols), lambda i: (jnp.maximum(i * per - 1, 0), 0))
    nxt = pl.BlockSpec((BF16_ROWS, cols), lambda i: (jnp.minimum((i + 1) * per, last), 0))
    return prev, nxt


def _ffn_core(x1_ext, g2, wup_ref, cw_ref, wdn_ref, g3, o_ref, act_ref, x1_ref,
              *, tiles_per_seq, chunk):
    tm, d_ff = act_ref.shape
    pos = pl.program_id(0) % tiles_per_seq
    x1_ref[...] = x1_ext[:tm]
    x1p = jnp.where(pos != 0, x1_ext[tm:tm + HALO], 0.0)
    x1n = jnp.where(pos != tiles_per_seq - 1, x1_ext[tm + HALO:], 0.0)
    h_ext = _rmsnorm(jnp.concatenate([x1_ref[...], x1p, x1n], axis=0), g2).astype(BF16)

    def up_branch(c0):
        p = _dot(h_ext, wup_ref[:, c0:c0 + chunk])
        return _dwconv3_ext(p, cw_ref[:, c0:c0 + chunk])

    for c0 in range(0, d_ff, chunk):
        u = up_branch(c0)
        gt = up_branch(d_ff + c0)
        act = jax.nn.gelu(gt, approximate=True) * u
        act_ref[:, c0:c0 + chunk] = act.astype(act_ref.dtype)
    half = tm // 2
    for r0 in (0, half):
        y = _dot(act_ref[r0:r0 + half], wdn_ref[...])
        o_ref[r0:r0 + half] = x1_ref[r0:r0 + half] + _rmsnorm(y, g3)


def _tail_even_kernel(a_ref, ap_ref, an_ref, c_ref, cp_ref, cn_ref, x_ref, xp_ref, xn_ref,
                      wa_ref, wc_ref, g1_ref, g2_ref, wup_ref, cw_ref, wdn_ref, g3_ref,
                      o_ref, act_ref, x1_ref, *, tiles_per_seq, chunk):
    def ext(m_ref, p_ref, n_ref):
        halo = jnp.concatenate([p_ref[...].astype(F32)[HALO:], n_ref[...].astype(F32)[:HALO]], axis=0)
        return jnp.concatenate([m_ref[...], halo.astype(m_ref.dtype)], axis=0)

    mix = _dot(ext(a_ref, ap_ref, an_ref), wa_ref[...]) + _dot(ext(c_ref, cp_ref, cn_ref), wc_ref[...])
    x_ext = jnp.concatenate([x_ref[...], xp_ref[HALO:], xn_ref[:HALO]], axis=0)
    x1_ext = x_ext + _rmsnorm(mix, g1_ref[...])
    _ffn_core(x1_ext, g2_ref[...], wup_ref, cw_ref, wdn_ref, g3_ref[...], o_ref, act_ref, x1_ref,
              tiles_per_seq=tiles_per_seq, chunk=chunk)


def _tail_odd_kernel(x_ref, xp_ref, xn_ref, g0_ref, pw_ref, ps_ref,
                     g1_ref, g2_ref, wup_ref, cw_ref, wdn_ref, g3_ref,
                     o_ref, act_ref, x1_ref, *, tiles_per_seq, chunk, seq):
    tm, d = x_ref.shape
    pos = pl.program_id(0) % tiles_per_seq
    xp = jnp.where(pos != 0, xp_ref[...], 0.0)
    xn = jnp.where(pos != tiles_per_seq - 1, xn_ref[...], 0.0)
    x_wide = jnp.concatenate([xp, x_ref[...], xn], axis=0)
    h_wide = _rmsnorm(x_wide, g0_ref[...])
    n_wide = tm + 2 * BF16_ROWS
    n_ext = tm + 2 * HALO
    group = d // len(POOL_WINDOWS)

    def reorder(a):
        return jnp.concatenate([a[HALO:HALO + tm], a[:HALO], a[HALO + tm:]], axis=0)

    t = pos * tm - HALO + lax.broadcasted_iota(jnp.int32, (n_ext, LANES), 0)
    outs = []
    for gi, w in enumerate(POOL_WINDOWS):
        hg = h_wide[:, gi * group:(gi + 1) * group]
        acc, span = hg, 1
        while span < w:
            acc = acc + pltpu.roll(acc, span, 0)
            span *= 2
        ahead = w // 2 - 1
        if ahead:
            acc = pltpu.roll(acc, n_wide - ahead, 0)
        win = acc[HALO:HALO + n_ext]
        cnt = jnp.minimum(t + w // 2, seq) - jnp.maximum(t - w // 2, 0)
        inv = 1.0 / jnp.maximum(cnt, 1).astype(F32)
        inv = jnp.concatenate([inv] * (group // LANES), axis=1)
        pooled = win * inv - hg[HALO:HALO + n_ext]
        outs.append(_dot(reorder(pooled).astype(BF16), pw_ref[gi]))
    mix = jnp.concatenate(outs, axis=1) * ps_ref[...]
    x1_ext = reorder(x_wide[HALO:HALO + n_ext]) + _rmsnorm(mix, g1_ref[...])
    _ffn_core(x1_ext, g2_ref[...], wup_ref, cw_ref, wdn_ref, g3_ref[...], o_ref, act_ref, x1_ref,
              tiles_per_seq=tiles_per_seq, chunk=chunk)


def _tail_call(body, row_inputs, row_specs, const_inputs, const_specs, x2d, d_ff, name):
    n_rows, d = x2d.shape
    tm = TM_FFN
    return pl.pallas_call(
        body,
        grid=(n_rows // tm,),
        in_specs=list(row_specs) + list(const_specs),
        out_specs=pl.BlockSpec((tm, d), lambda i: (i, 0)),
        out_shape=jax.ShapeDtypeStruct(x2d.shape, x2d.dtype),
        scratch_shapes=[pltpu.VMEM((tm, d_ff), BF16), pltpu.VMEM((tm, d), F32)],
        compiler_params=pltpu.CompilerParams(
            dimension_semantics=("parallel",), vmem_limit_bytes=VMEM_LIMIT),
        name=name,
    )(*row_inputs, *const_inputs)


def _tail_even(attn, conv, x2d, w_out, e, g_all, g_row, w_up, conv_w, w_down, layer, *, seq):
    n_rows, d = x2d.shape
    tm = TM_FFN
    row = lambda cols: pl.BlockSpec((tm, cols), lambda i: (i, 0))
    a_prev, a_next = _halo16_specs(tm, n_rows, ATTN_DIM)
    c_prev, c_next = _halo16_specs(tm, n_rows, CONV_DIM)
    x_prev, x_next = _halo16_specs(tm, n_rows, d)
    body = functools.partial(_tail_even_kernel, tiles_per_seq=seq // tm, chunk=FFN_CHUNK)
    return _tail_call(
        body,
        (attn, attn, attn, conv, conv, conv, x2d, x2d, x2d),
        (row(ATTN_DIM), a_prev, a_next, row(CONV_DIM), c_prev, c_next, row(d), x_prev, x_next),
        (w_out, w_out, g_all, g_all, w_up, conv_w, w_down, g_all),
        (_layer_spec(w_out, e, (ATTN_DIM, d), (0, 0)),
         _layer_spec(w_out, e, (CONV_DIM, d), (ATTN_DIM // CONV_DIM, 0)),
         _layer_spec(g_all, g_row + 1), _layer_spec(g_all, g_row + 2),
         _layer_spec(w_up, layer), _layer_spec(conv_w, layer), _layer_spec(w_down, layer),
         _layer_spec(g_all, g_row + 3)),
        x2d, w_down.shape[1], "tail_even")


def _tail_odd(x2d, pool_w, pool_scale, o, g_all, g_row, w_up, conv_w, w_down, layer, *, seq):
    n_rows, d = x2d.shape
    tm = TM_FFN
    x_prev, x_next = _halo16_specs(tm, n_rows, d)
    body = functools.partial(_tail_odd_kernel, tiles_per_seq=seq // tm, chunk=FFN_CHUNK, seq=seq)
    return _tail_call(
        body,
        (x2d, x2d, x2d),
        (pl.BlockSpec((tm, d), lambda i: (i, 0)), x_prev, x_next),
        (g_all, pool_w, pool_scale, g_all, g_all, w_up, conv_w, w_down, g_all),
        (_layer_spec(g_all, g_row), _layer_spec(pool_w, o), _layer_spec(pool_scale, o),
         _layer_spec(g_all, g_row + 1), _layer_spec(g_all, g_row + 2),
         _layer_spec(w_up, layer), _layer_spec(conv_w, layer), _layer_spec(w_down, layer),
         _layer_spec(g_all, g_row + 3)),
        x2d, w_down.shape[1], "tail_odd")


def _rope_tables(seq):
    rows = seq // GRID_W
    row = jnp.repeat(jnp.arange(rows, dtype=F32), GRID_W)
    col = jnp.tile(jnp.arange(GRID_W, dtype=F32), rows)
    inv = ROPE_THETA ** (-jnp.arange(0, ROPE_AXIS_DIM, 2, dtype=F32) / ROPE_AXIS_DIM)
    ang_r = row[:, None] * inv[None, :]
    ang_c = col[:, None] * inv[None, :]
    cos_r, sin_r, cos_c, sin_c = jnp.cos(ang_r), jnp.sin(ang_r), jnp.cos(ang_c), jnp.sin(ang_c)
    zero = jnp.zeros_like(sin_r)
    c = jnp.concatenate([cos_r, cos_r, cos_c, cos_c], axis=1)
    su = jnp.concatenate([zero, sin_r, zero, sin_c], axis=1)
    sd = jnp.concatenate([-sin_r, zero, -sin_c, zero], axis=1)
    reps = LANES // HEAD_DIM
    return tuple(jnp.tile(t, (1, reps)) for t in (c, su, sd))


def kernel(x, norm_g, mix_w_in, q_norm_g, k_norm_g, sconv_w, mix_w_out, pool_w, pool_scale,
           ffn_w_up, ffn_conv_w, ffn_w_down):
    batch, seq, d = x.shape
    depth, norms_per_layer = norm_g.shape[:2]
    assert seq % TM_PROJ == 0 and seq % TM_FFN == 0 and seq % TQ_ATTN == 0
    assert ffn_w_down.shape[1] % FFN_CHUNK == 0

    c_tab, su_tab, sd_tab = _rope_tables(seq)
    lane_head = jnp.arange(MXU_DIM) // HEAD_DIM
    bd = jnp.where(lane_head[:, None] == lane_head[None, :], 1.0 / HEAD_DIM, 0.0).astype(BF16)

    w_in, w_out, pool_w_bf = mix_w_in.astype(BF16), mix_w_out.astype(BF16), pool_w.astype(BF16)
    w_up, w_down = ffn_w_up.astype(BF16), ffn_w_down.astype(BF16)
    g_all = norm_g.reshape(depth * norms_per_layer, 1, d)
    pool_scale3 = pool_scale[:, None, :]

    x2d = x.reshape(batch * seq, d)
    for i in range(depth):
        g_row = i * norms_per_layer
        if i % 2 == 0:
            e = i // 2
            qg = jnp.tile(q_norm_g[e], N_Q_HEADS)[None, :]
            kg = jnp.tile(k_norm_g[e], N_KV_HEADS)[None, :]
            qt, ka, kb, vt, conv = _inproj(
                x2d, g_all, g_row, w_in, e, qg, kg, bd, c_tab, su_tab, sd_tab, sconv_w, seq=seq)
            attn = _attention(qt, ka, kb, vt, batch=batch, seq=seq)
            x2d = _tail_even(attn, conv, x2d, w_out, e, g_all, g_row, w_up, ffn_conv_w, w_down, i,
                             seq=seq)
        else:
            x2d = _tail_odd(x2d, pool_w_bf, pool_scale3, i // 2, g_all, g_row, w_up, ffn_conv_w,
                            w_down, i, seq=seq)
    return x2d.reshape(batch, seq, d)
```

```python
import functools
import math

import jax
import jax.numpy as jnp
from jax import lax
from jax.experimental import pallas as pl
from jax.experimental.pallas import tpu as pltpu

F32 = jnp.float32
BF16 = jnp.bfloat16

GRID_W = 64
HEAD_DIM = 64
N_Q_HEADS = 8
N_KV_HEADS = 2
Q_GROUP = N_Q_HEADS // N_KV_HEADS
ATTN_DIM = N_Q_HEADS * HEAD_DIM
KV_DIM = N_KV_HEADS * HEAD_DIM
CONV_DIM = 512
ROPE_THETA = 10000.0
ROPE_AXIS_DIM = HEAD_DIM // 2
POOL_WINDOWS = (2, 4, 8, 16)
RMS_EPS = 1e-6

LANES = 128
SUBLANES = 8
BF16_ROWS = 16
HALO = SUBLANES
MXU_DIM = 256

VMEM_LIMIT = 56 * 1024 * 1024

TM_PROJ = 512
TM_FFN = 512
TQ_ATTN = 2048
ATTN_KEY_BLOCK = 256
ATTN_Q_BLOCK = 256
ATTN_LOOKAHEAD = 6
FFN_CHUNK = 256


def _rmsnorm(x, g):
    ms = jnp.mean(x * x, axis=-1, keepdims=True)
    return x * lax.rsqrt(ms + RMS_EPS) * g


def _dot(a, b):
    return jnp.dot(a, b, preferred_element_type=F32)


def _const_spec(shape):
    zeros = (0,) * len(shape)
    return pl.BlockSpec(shape, lambda *_: zeros, pipeline_mode=pl.Buffered(1))


def _layer_spec(arr, layer, block=None, pos=None):
    block = tuple(arr.shape[1:]) if block is None else tuple(block)
    pos = (0,) * len(block) if pos is None else tuple(pos)
    index = (layer,) + pos
    return pl.BlockSpec((None,) + block, lambda *_: index, pipeline_mode=pl.Buffered(1))


def _halo_specs(tm, n_rows, d):
    per = tm // HALO
    last = n_rows // HALO - 1
    main = pl.BlockSpec((tm, d), lambda i: (i, 0))
    prev = pl.BlockSpec((HALO, d), lambda i: (jnp.maximum(i * per - 1, 0), 0))
    nxt = pl.BlockSpec((HALO, d), lambda i: (jnp.minimum((i + 1) * per, last), 0))
    return main, prev, nxt


def _load_ext(x_ref, xp_ref, xn_ref, tiles_per_seq):
    pos = pl.program_id(0) % tiles_per_seq
    xp = jnp.where(pos != 0, xp_ref[...], 0.0)
    xn = jnp.where(pos != tiles_per_seq - 1, xn_ref[...], 0.0)
    return jnp.concatenate([x_ref[...], xp, xn], axis=0)


def _shift_rows(pm, prev_row, next_row):
    tm = pm.shape[0]
    r = lax.broadcasted_iota(jnp.int32, (SUBLANES, pm.shape[1]), 0)
    m1 = pltpu.roll(pm, 1, 0)
    p1 = pltpu.roll(pm, tm - 1, 0)
    m1 = jnp.concatenate(
        [jnp.where(r == 0, prev_row, m1[:SUBLANES]), m1[SUBLANES:]], axis=0)
    p1 = jnp.concatenate(
        [p1[:tm - SUBLANES], jnp.where(r == SUBLANES - 1, next_row, p1[tm - SUBLANES:])], axis=0)
    return m1, p1


def _dwconv3_ext(p, cw):
    tm = p.shape[0] - 2 * HALO
    pm = p[:tm]
    m1, p1 = _shift_rows(pm, p[tm + HALO - 1:tm + HALO], p[tm + HALO:tm + HALO + 1])
    return m1 * cw[0:1] + pm * cw[1:2] + p1 * cw[2:3]


def _head_mean_square(v, bd):
    sq = v * v
    hi = sq.astype(BF16)
    lo = (sq - hi.astype(F32)).astype(BF16)
    outs = []
    for c0 in range(0, v.shape[1], MXU_DIM):
        wd = min(MXU_DIM, v.shape[1] - c0)
        m = bd[:wd, :wd]
        outs.append(_dot(hi[:, c0:c0 + wd], m) + _dot(lo[:, c0:c0 + wd], m))
    return outs[0] if len(outs) == 1 else jnp.concatenate(outs, axis=1)


def _rope(v, c, su, sd):
    outs = []
    for s in range(v.shape[1] // LANES):
        vs = v[:, s * LANES:(s + 1) * LANES]
        up = pltpu.roll(vs, ROPE_AXIS_DIM // 2, 1)
        dn = pltpu.roll(vs, LANES - ROPE_AXIS_DIM // 2, 1)
        outs.append(vs * c + up * su + dn * sd)
    return outs[0] if len(outs) == 1 else jnp.concatenate(outs, axis=1)


def _inproj_kernel(x_ref, xp_ref, xn_ref, g_ref, w_ref, qg_ref, kg_ref, bd_ref,
                   c_ref, su_ref, sd_ref, cw_ref,
                   qt_out, ka_out, kb_out, vt_out, conv_out,
                   *, tm, tiles_per_seq, q_scale):
    x_ext = _load_ext(x_ref, xp_ref, xn_ref, tiles_per_seq)
    h_ext = _rmsnorm(x_ext, g_ref[...]).astype(BF16)
    n_head_cols = ATTN_DIM + 2 * KV_DIM + CONV_DIM
    main = _dot(h_ext[:tm], w_ref[:, :n_head_cols])

    bd = bd_ref[...]
    c, su, sd = c_ref[...], su_ref[...], sd_ref[...]

    q_raw = main[:, :ATTN_DIM]
    qn = q_raw * lax.rsqrt(_head_mean_square(q_raw, bd) + RMS_EPS) * qg_ref[...]
    qt_out[...] = (_rope(qn, c, su, sd) * q_scale).T.astype(qt_out.dtype)
    vt_out[...] = main[:, ATTN_DIM + KV_DIM:ATTN_DIM + 2 * KV_DIM].T.astype(vt_out.dtype)

    k_raw = main[:, ATTN_DIM:ATTN_DIM + KV_DIM]
    kn = k_raw * lax.rsqrt(_head_mean_square(k_raw, bd) + RMS_EPS) * kg_ref[...]
    kr = _rope(kn, c, su, sd)
    sw = pltpu.roll(kr, HEAD_DIM, 1)
    low = lax.broadcasted_iota(jnp.int32, (tm, LANES), 1) < HEAD_DIM
    ka_out[0] = jnp.where(low, kr, 0.0).astype(ka_out.dtype)
    kb_out[0] = jnp.where(low, 0.0, sw).astype(kb_out.dtype)
    ka_out[1] = jnp.where(low, sw, 0.0).astype(ka_out.dtype)
    kb_out[1] = jnp.where(low, 0.0, kr).astype(kb_out.dtype)

    gate_b = main[:, ATTN_DIM + 2 * KV_DIM:]
    for c0 in range(0, CONV_DIM, MXU_DIM):
        gate_c = _dot(h_ext, w_ref[:, n_head_cols + c0:n_head_cols + c0 + MXU_DIM])
        conv_in = _dot(h_ext, w_ref[:, n_head_cols + CONV_DIM + c0:n_head_cols + CONV_DIM + c0 + MXU_DIM])
        conv = _dwconv3_ext(gate_c * conv_in, cw_ref[:, c0:c0 + MXU_DIM])
        conv_out[:, c0:c0 + MXU_DIM] = (gate_b[:, c0:c0 + MXU_DIM] * conv).astype(conv_out.dtype)


def _inproj(x2d, g_all, g_row, w_in, layer, qg, kg, bd, c_tab, su_tab, sd_tab, sconv_w, *, seq):
    n_rows, d = x2d.shape
    tm = TM_PROJ
    tiles_per_seq = seq // tm
    main, prev, nxt = _halo_specs(tm, n_rows, d)
    tab = pl.BlockSpec((tm, LANES), lambda i: (i % tiles_per_seq, 0))
    k_spec = pl.BlockSpec((N_KV_HEADS, tm, LANES), lambda i: (0, i, 0))
    k_shape = jax.ShapeDtypeStruct((N_KV_HEADS, n_rows, LANES), BF16)
    q_scale = HEAD_DIM ** -0.5 * math.log2(math.e)
    return pl.pallas_call(
        functools.partial(_inproj_kernel, tm=tm, tiles_per_seq=tiles_per_seq, q_scale=q_scale),
        grid=(n_rows // tm,),
        in_specs=[main, prev, nxt, _layer_spec(g_all, g_row), _layer_spec(w_in, layer),
                  _const_spec(qg.shape), _const_spec(kg.shape), _const_spec(bd.shape),
                  tab, tab, tab, _layer_spec(sconv_w, layer)],
        out_specs=[pl.BlockSpec((ATTN_DIM, tm), lambda i: (0, i)),
                   k_spec, k_spec,
                   pl.BlockSpec((KV_DIM, tm), lambda i: (0, i)),
                   pl.BlockSpec((tm, CONV_DIM), lambda i: (i, 0))],
        out_shape=[jax.ShapeDtypeStruct((ATTN_DIM, n_rows), BF16),
                   k_shape, k_shape,
                   jax.ShapeDtypeStruct((KV_DIM, n_rows), BF16),
                   jax.ShapeDtypeStruct((n_rows, CONV_DIM), BF16)],
        compiler_params=pltpu.CompilerParams(
            dimension_semantics=("parallel",), vmem_limit_bytes=VMEM_LIMIT),
        name="inproj",
    )(x2d, x2d, x2d, g_all, w_in, qg, kg, bd, c_tab, su_tab, sd_tab, sconv_w)


def _attn_kernel(qt_ref, ka_ref, kb_ref, vt_ref, o_ref):
    seq = vt_ref.shape[1]
    tq = qt_ref.shape[1]
    k_refs = (ka_ref, kb_ref, ka_ref, kb_ref)
    q_cols = [slice(c0, c0 + ATTN_Q_BLOCK) for c0 in range(0, tq, ATTN_Q_BLOCK)]
    units = [(g, k0, qi) for g in range(Q_GROUP) for k0 in range(0, seq, ATTN_KEY_BLOCK)
             for qi in range(len(q_cols))]

    def scores(u):
        g, k0, qi = units[u]
        slab = g // 2
        return _dot(k_refs[g][k0:k0 + ATTN_KEY_BLOCK, :],
                    qt_ref[slab * LANES:(slab + 1) * LANES, q_cols[qi]])

    pending = [scores(u) for u in range(min(ATTN_LOOKAHEAD, len(units)))]
    ones = jnp.ones((BF16_ROWS, ATTN_KEY_BLOCK), BF16)
    outs = [[None] * len(q_cols) for _ in range(Q_GROUP)]
    m = [None] * len(q_cols)
    acc = [None] * len(q_cols)
    for u, (g, k0, qi) in enumerate(units):
        st = pending.pop(0)
        if u + ATTN_LOOKAHEAD < len(units):
            pending.append(scores(u + ATTN_LOOKAHEAD))
        vt1 = jnp.concatenate([vt_ref[:, k0:k0 + ATTN_KEY_BLOCK], ones], axis=0)
        blk_max = jnp.max(st, axis=0, keepdims=True)
        if k0 == 0:
            m[qi] = blk_max
            acc[qi] = _dot(vt1, jnp.exp2(st - blk_max).astype(BF16))
        else:
            m_new = jnp.maximum(m[qi], blk_max)
            acc[qi] = (jnp.exp2(m[qi] - m_new) * acc[qi]
                       + _dot(vt1, jnp.exp2(st - m_new).astype(BF16)))
            m[qi] = m_new
        if k0 + ATTN_KEY_BLOCK == seq:
            outs[g][qi] = acc[qi][:HEAD_DIM] * (1.0 / acc[qi][HEAD_DIM:HEAD_DIM + 1])
    o_t = jnp.concatenate([jnp.concatenate(row, axis=1) for row in outs], axis=0)
    o_ref[...] = o_t.T.astype(o_ref.dtype)


def _attention(qt, ka, kb, vt, *, batch, seq):
    tq = TQ_ATTN
    q_tiles = seq // tq
    group_cols = Q_GROUP * HEAD_DIM
    k_spec = pl.BlockSpec((None, seq, LANES), lambda b, h, i: (h, b, 0))
    return pl.pallas_call(
        _attn_kernel,
        grid=(batch, N_KV_HEADS, q_tiles),
        in_specs=[pl.BlockSpec((group_cols, tq), lambda b, h, i: (h, b * q_tiles + i)),
                  k_spec, k_spec,
                  pl.BlockSpec((HEAD_DIM, seq), lambda b, h, i: (h, b))],
        out_specs=pl.BlockSpec((tq, group_cols), lambda b, h, i: (b * q_tiles + i, h)),
        out_shape=jax.ShapeDtypeStruct((qt.shape[1], qt.shape[0]), BF16),
        compiler_params=pltpu.CompilerParams(
            dimension_semantics=("parallel", "parallel", "arbitrary"),
            vmem_limit_bytes=VMEM_LIMIT),
        name="attention",
    )(qt, ka, kb, vt)


def _halo16_specs(tm, n_rows, cols):
    per = tm // BF16_ROWS
    last = n_rows // BF16_ROWS - 1
    prev = pl.BlockSpec((BF16_ROWS, cols), lambda i: (jnp.maximum(i * per - 1, 0), 0))
    nxt = pl.BlockSpec((BF16_ROWS, cols), lambda i: (jnp.minimum((i + 1) * per, last), 0))
    return prev, nxt


def _ffn_core(x1_ext, g2, wup_ref, cw_ref, wdn_ref, g3, o_ref, act_ref, x1_ref,
              *, tiles_per_seq, chunk):
    tm, d_ff = act_ref.shape
    pos = pl.program_id(0) % tiles_per_seq
    x1_ref[...] = x1_ext[:tm]
    x1p = jnp.where(pos != 0, x1_ext[tm:tm + HALO], 0.0)
    x1n = jnp.where(pos != tiles_per_seq - 1, x1_ext[tm + HALO:], 0.0)
    h_ext = _rmsnorm(jnp.concatenate([x1_ref[...], x1p, x1n], axis=0), g2).astype(BF16)

    def up_branch(c0):
        p = _dot(h_ext, wup_ref[:, c0:c0 + chunk])
        return _dwconv3_ext(p, cw_ref[:, c0:c0 + chunk])

    for c0 in range(0, d_ff, chunk):
        u = up_branch(c0)
        gt = up_branch(d_ff + c0)
        act = jax.nn.gelu(gt, approximate=True) * u
        act_ref[:, c0:c0 + chunk] = act.astype(act_ref.dtype)
    half = tm // 2
    for r0 in (0, half):
        y = _dot(act_ref[r0:r0 + half], wdn_ref[...])
        o_ref[r0:r0 + half] = x1_ref[r0:r0 + half] + _rmsnorm(y, g3)


def _tail_even_kernel(a_ref, ap_ref, an_ref, c_ref, cp_ref, cn_ref, x_ref, xp_ref, xn_ref,
                      wa_ref, wc_ref, g1_ref, g2_ref, wup_ref, cw_ref, wdn_ref, g3_ref,
                      o_ref, act_ref, x1_ref, *, tiles_per_seq, chunk):
    def ext(m_ref, p_ref, n_ref):
        halo = jnp.concatenate([p_ref[...].astype(F32)[HALO:], n_ref[...].astype(F32)[:HALO]], axis=0)
        return jnp.concatenate([m_ref[...], halo.astype(m_ref.dtype)], axis=0)

    a_ext, c_ext = ext(a_ref, ap_ref, an_ref), ext(c_ref, cp_ref, cn_ref)
    x_ext = jnp.concatenate([x_ref[...], xp_ref[HALO:], xn_ref[:HALO]], axis=0)
    split = x_ref.shape[0] // 2
    parts = []
    for r in (slice(0, split), slice(split, x_ext.shape[0])):
        mix = _dot(a_ext[r], wa_ref[...]) + _dot(c_ext[r], wc_ref[...])
        parts.append(x_ext[r] + _rmsnorm(mix, g1_ref[...]))
    x1_ext = jnp.concatenate(parts, axis=0)
    _ffn_core(x1_ext, g2_ref[...], wup_ref, cw_ref, wdn_ref, g3_ref[...], o_ref, act_ref, x1_ref,
              tiles_per_seq=tiles_per_seq, chunk=chunk)


def _tail_odd_kernel(x_ref, xp_ref, xn_ref, g0_ref, pw_ref, ps_ref,
                     g1_ref, g2_ref, wup_ref, cw_ref, wdn_ref, g3_ref,
                     o_ref, act_ref, x1_ref, *, tiles_per_seq, chunk, seq):
    tm, d = x_ref.shape
    pos = pl.program_id(0) % tiles_per_seq
    xp = jnp.where(pos != 0, xp_ref[...], 0.0)
    xn = jnp.where(pos != tiles_per_seq - 1, xn_ref[...], 0.0)
    x_wide = jnp.concatenate([xp, x_ref[...], xn], axis=0)
    h_wide = _rmsnorm(x_wide, g0_ref[...])
    n_wide = tm + 2 * BF16_ROWS
    n_ext = tm + 2 * HALO
    group = d // len(POOL_WINDOWS)

    def reorder(a):
        return jnp.concatenate([a[HALO:HALO + tm], a[:HALO], a[HALO + tm:]], axis=0)

    t = pos * tm - HALO + lax.broadcasted_iota(jnp.int32, (n_ext, LANES), 0)
    outs = []
    for gi, w in enumerate(POOL_WINDOWS):
        hg = h_wide[:, gi * group:(gi + 1) * group]
        acc, span = hg, 1
        while span < w:
            acc = acc + pltpu.roll(acc, span, 0)
            span *= 2
        ahead = w // 2 - 1
        if ahead:
            acc = pltpu.roll(acc, n_wide - ahead, 0)
        win = acc[HALO:HALO + n_ext]
        cnt = jnp.minimum(t + w // 2, seq) - jnp.maximum(t - w // 2, 0)
        inv = 1.0 / jnp.maximum(cnt, 1).astype(F32)
        inv = jnp.concatenate([inv] * (group // LANES), axis=1)
        pooled = win * inv - hg[HALO:HALO + n_ext]
        outs.append(_dot(reorder(pooled).astype(BF16), pw_ref[gi]))
    mix = jnp.concatenate(outs, axis=1) * ps_ref[...]
    x1_ext = reorder(x_wide[HALO:HALO + n_ext]) + _rmsnorm(mix, g1_ref[...])
    _ffn_core(x1_ext, g2_ref[...], wup_ref, cw_ref, wdn_ref, g3_ref[...], o_ref, act_ref, x1_ref,
              tiles_per_seq=tiles_per_seq, chunk=chunk)


def _tail_call(body, row_inputs, row_specs, const_inputs, const_specs, x2d, d_ff, name):
    n_rows, d = x2d.shape
    tm = TM_FFN
    return pl.pallas_call(
        body,
        grid=(n_rows // tm,),
        in_specs=list(row_specs) + list(const_specs),
        out_specs=pl.BlockSpec((tm, d), lambda i: (i, 0)),
        out_shape=jax.ShapeDtypeStruct(x2d.shape, x2d.dtype),
        scratch_shapes=[pltpu.VMEM((tm, d_ff), BF16), pltpu.VMEM((tm, d), F32)],
        compiler_params=pltpu.CompilerParams(
            dimension_semantics=("parallel",), vmem_limit_bytes=VMEM_LIMIT),
        name=name,
    )(*row_inputs, *const_inputs)


def _tail_even(attn, conv, x2d, w_out, e, g_all, g_row, w_up, conv_w, w_down, layer, *, seq):
    n_rows, d = x2d.shape
    tm = TM_FFN
    row = lambda cols: pl.BlockSpec((tm, cols), lambda i: (i, 0))
    a_prev, a_next = _halo16_specs(tm, n_rows, ATTN_DIM)
    c_prev, c_next = _halo16_specs(tm, n_rows, CONV_DIM)
    x_prev, x_next = _halo16_specs(tm, n_rows, d)
    body = functools.partial(_tail_even_kernel, tiles_per_seq=seq // tm, chunk=FFN_CHUNK)
    return _tail_call(
        body,
        (attn, attn, attn, conv, conv, conv, x2d, x2d, x2d),
        (row(ATTN_DIM), a_prev, a_next, row(CONV_DIM), c_prev, c_next, row(d), x_prev, x_next),
        (w_out, w_out, g_all, g_all, w_up, conv_w, w_down, g_all),
        (_layer_spec(w_out, e, (ATTN_DIM, d), (0, 0)),
         _layer_spec(w_out, e, (CONV_DIM, d), (ATTN_DIM // CONV_DIM, 0)),
         _layer_spec(g_all, g_row + 1), _layer_spec(g_all, g_row + 2),
         _layer_spec(w_up, layer), _layer_spec(conv_w, layer), _layer_spec(w_down, layer),
         _layer_spec(g_all, g_row + 3)),
        x2d, w_down.shape[1], "tail_even")


def _tail_odd(x2d, pool_w, pool_scale, o, g_all, g_row, w_up, conv_w, w_down, layer, *, seq):
    n_rows, d = x2d.shape
    tm = TM_FFN
    x_prev, x_next = _halo16_specs(tm, n_rows, d)
    body = functools.partial(_tail_odd_kernel, tiles_per_seq=seq // tm, chunk=FFN_CHUNK, seq=seq)
    return _tail_call(
        body,
        (x2d, x2d, x2d),
        (pl.BlockSpec((tm, d), lambda i: (i, 0)), x_prev, x_next),
        (g_all, pool_w, pool_scale, g_all, g_all, w_up, conv_w, w_down, g_all),
        (_layer_spec(g_all, g_row), _layer_spec(pool_w, o), _layer_spec(pool_scale, o),
         _layer_spec(g_all, g_row + 1), _layer_spec(g_all, g_row + 2),
         _layer_spec(w_up, layer), _layer_spec(conv_w, layer), _layer_spec(w_down, layer),
         _layer_spec(g_all, g_row + 3)),
        x2d, w_down.shape[1], "tail_odd")


def _rope_tables(seq):
    rows = seq // GRID_W
    row = jnp.repeat(jnp.arange(rows, dtype=F32), GRID_W)
    col = jnp.tile(jnp.arange(GRID_W, dtype=F32), rows)
    inv = ROPE_THETA ** (-jnp.arange(0, ROPE_AXIS_DIM, 2, dtype=F32) / ROPE_AXIS_DIM)
    ang_r = row[:, None] * inv[None, :]
    ang_c = col[:, None] * inv[None, :]
    cos_r, sin_r, cos_c, sin_c = jnp.cos(ang_r), jnp.sin(ang_r), jnp.cos(ang_c), jnp.sin(ang_c)
    zero = jnp.zeros_like(sin_r)
    c = jnp.concatenate([cos_r, cos_r, cos_c, cos_c], axis=1)
    su = jnp.concatenate([zero, sin_r, zero, sin_c], axis=1)
    sd = jnp.concatenate([-sin_r, zero, -sin_c, zero], axis=1)
    reps = LANES // HEAD_DIM
    return tuple(jnp.tile(t, (1, reps)) for t in (c, su, sd))


def kernel(x, norm_g, mix_w_in, q_norm_g, k_norm_g, sconv_w, mix_w_out, pool_w, pool_scale,
           ffn_w_up, ffn_conv_w, ffn_w_down):
    batch, seq, d = x.shape
    depth, norms_per_layer = norm_g.shape[:2]
    assert seq % TM_PROJ == 0 and seq % TM_FFN == 0 and seq % TQ_ATTN == 0
    assert ffn_w_down.shape[1] % FFN_CHUNK == 0

    c_tab, su_tab, sd_tab = _rope_tables(seq)
    lane_head = jnp.arange(MXU_DIM) // HEAD_DIM
    bd = jnp.where(lane_head[:, None] == lane_head[None, :], 1.0 / HEAD_DIM, 0.0).astype(BF16)

    w_in, w_out, pool_w_bf = mix_w_in.astype(BF16), mix_w_out.astype(BF16), pool_w.astype(BF16)
    w_up, w_down = ffn_w_up.astype(BF16), ffn_w_down.astype(BF16)
    g_all = norm_g.reshape(depth * norms_per_layer, 1, d)
    pool_scale3 = pool_scale[:, None, :]

    x2d = x.reshape(batch * seq, d)
    for i in range(depth):
        g_row = i * norms_per_layer
        if i % 2 == 0:
            e = i // 2
            qg = jnp.tile(q_norm_g[e], N_Q_HEADS)[None, :]
            kg = jnp.tile(k_norm_g[e], N_KV_HEADS)[None, :]
            qt, ka, kb, vt, conv = _inproj(
                x2d, g_all, g_row, w_in, e, qg, kg, bd, c_tab, su_tab, sd_tab, sconv_w, seq=seq)
            attn = _attention(qt, ka, kb, vt, batch=batch, seq=seq)
            x2d = _tail_even(attn, conv, x2d, w_out, e, g_all, g_row, w_up, ffn_conv_w, w_down, i,
                             seq=seq)
        else:
            x2d = _tail_odd(x2d, pool_w_bf, pool_scale3, i // 2, g_all, g_row, w_up, ffn_conv_w,
                            w_down, i, seq=seq)
    return x2d.reshape(batch, seq, d)
```

```python
import functools
import math

import jax
import jax.numpy as jnp
from jax import lax
from jax.experimental import pallas as pl
from jax.experimental.pallas import tpu as pltpu

F32 = jnp.float32
BF16 = jnp.bfloat16

GRID_W = 64
HEAD_DIM = 64
N_Q_HEADS = 8
N_KV_HEADS = 2
Q_GROUP = N_Q_HEADS // N_KV_HEADS
ATTN_DIM = N_Q_HEADS * HEAD_DIM
KV_DIM = N_KV_HEADS * HEAD_DIM
CONV_DIM = 512
ROPE_THETA = 10000.0
ROPE_AXIS_DIM = HEAD_DIM // 2
POOL_WINDOWS = (2, 4, 8, 16)
RMS_EPS = 1e-6

LANES = 128
SUBLANES = 8
BF16_ROWS = 16
HALO = SUBLANES
MXU_DIM = 256

VMEM_LIMIT = 56 * 1024 * 1024

TM_PROJ = 512
TM_FFN = 512
TQ_ATTN = 2048
ATTN_KEY_BLOCK = 256
ATTN_Q_BLOCK = 256
ATTN_LOOKAHEAD = 6
FFN_CHUNK = 256


def _rmsnorm(x, g):
    ms = jnp.mean(x * x, axis=-1, keepdims=True)
    return x * lax.rsqrt(ms + RMS_EPS) * g


def _dot(a, b):
    return jnp.dot(a, b, preferred_element_type=F32)


def _const_spec(shape):
    zeros = (0,) * len(shape)
    return pl.BlockSpec(shape, lambda *_: zeros, pipeline_mode=pl.Buffered(1))


def _layer_spec(arr, layer, block=None, pos=None):
    block = tuple(arr.shape[1:]) if block is None else tuple(block)
    pos = (0,) * len(block) if pos is None else tuple(pos)
    index = (layer,) + pos
    return pl.BlockSpec((None,) + block, lambda *_: index, pipeline_mode=pl.Buffered(1))


def _halo_specs(tm, n_rows, d):
    per = tm // HALO
    last = n_rows // HALO - 1
    main = pl.BlockSpec((tm, d), lambda i: (i, 0))
    prev = pl.BlockSpec((HALO, d), lambda i: (jnp.maximum(i * per - 1, 0), 0))
    nxt = pl.BlockSpec((HALO, d), lambda i: (jnp.minimum((i + 1) * per, last), 0))
    return main, prev, nxt


def _load_ext(x_ref, xp_ref, xn_ref, tiles_per_seq):
    pos = pl.program_id(0) % tiles_per_seq
    xp = jnp.where(pos != 0, xp_ref[...], 0.0)
    xn = jnp.where(pos != tiles_per_seq - 1, xn_ref[...], 0.0)
    return jnp.concatenate([x_ref[...], xp, xn], axis=0)


def _shift_rows(pm, prev_row, next_row):
    tm = pm.shape[0]
    r = lax.broadcasted_iota(jnp.int32, (SUBLANES, pm.shape[1]), 0)
    m1 = pltpu.roll(pm, 1, 0)
    p1 = pltpu.roll(pm, tm - 1, 0)
    m1 = jnp.concatenate(
        [jnp.where(r == 0, prev_row, m1[:SUBLANES]), m1[SUBLANES:]], axis=0)
    p1 = jnp.concatenate(
        [p1[:tm - SUBLANES], jnp.where(r == SUBLANES - 1, next_row, p1[tm - SUBLANES:])], axis=0)
    return m1, p1


def _dwconv3_ext(p, cw):
    tm = p.shape[0] - 2 * HALO
    pm = p[:tm]
    m1, p1 = _shift_rows(pm, p[tm + HALO - 1:tm + HALO], p[tm + HALO:tm + HALO + 1])
    return m1 * cw[0:1] + pm * cw[1:2] + p1 * cw[2:3]


def _head_mean_square(v, bd):
    sq = v * v
    hi = sq.astype(BF16)
    lo = (sq - hi.astype(F32)).astype(BF16)
    outs = []
    for c0 in range(0, v.shape[1], MXU_DIM):
        wd = min(MXU_DIM, v.shape[1] - c0)
        m = bd[:wd, :wd]
        outs.append(_dot(hi[:, c0:c0 + wd], m) + _dot(lo[:, c0:c0 + wd], m))
    return outs[0] if len(outs) == 1 else jnp.concatenate(outs, axis=1)


def _rope(v, c, su, sd):
    outs = []
    for s in range(v.shape[1] // LANES):
        vs = v[:, s * LANES:(s + 1) * LANES]
        up = pltpu.roll(vs, ROPE_AXIS_DIM // 2, 1)
        dn = pltpu.roll(vs, LANES - ROPE_AXIS_DIM // 2, 1)
        outs.append(vs * c + up * su + dn * sd)
    return outs[0] if len(outs) == 1 else jnp.concatenate(outs, axis=1)


def _inproj_kernel(x_ref, xp_ref, xn_ref, g_ref, w_ref, qg_ref, kg_ref, bd_ref,
                   c_ref, su_ref, sd_ref, cw_ref,
                   qt_out, ka_out, kb_out, vt_out, conv_out,
                   *, tm, tiles_per_seq, q_scale):
    x_ext = _load_ext(x_ref, xp_ref, xn_ref, tiles_per_seq)
    h_ext = _rmsnorm(x_ext, g_ref[...]).astype(BF16)
    n_head_cols = ATTN_DIM + 2 * KV_DIM + CONV_DIM
    main = _dot(h_ext[:tm], w_ref[:, :n_head_cols])

    bd = bd_ref[...]
    c, su, sd = c_ref[...], su_ref[...], sd_ref[...]

    q_raw = main[:, :ATTN_DIM]
    qn = q_raw * lax.rsqrt(_head_mean_square(q_raw, bd) + RMS_EPS) * qg_ref[...]
    qt_out[...] = (_rope(qn, c, su, sd) * q_scale).T.astype(qt_out.dtype)
    vt_out[...] = main[:, ATTN_DIM + KV_DIM:ATTN_DIM + 2 * KV_DIM].T.astype(vt_out.dtype)

    k_raw = main[:, ATTN_DIM:ATTN_DIM + KV_DIM]
    kn = k_raw * lax.rsqrt(_head_mean_square(k_raw, bd) + RMS_EPS) * kg_ref[...]
    kr = _rope(kn, c, su, sd)
    sw = pltpu.roll(kr, HEAD_DIM, 1)
    low = lax.broadcasted_iota(jnp.int32, (tm, LANES), 1) < HEAD_DIM
    ka_out[0] = jnp.where(low, kr, 0.0).astype(ka_out.dtype)
    kb_out[0] = jnp.where(low, 0.0, sw).astype(kb_out.dtype)
    ka_out[1] = jnp.where(low, sw, 0.0).astype(ka_out.dtype)
    kb_out[1] = jnp.where(low, 0.0, kr).astype(kb_out.dtype)

    gate_b = main[:, ATTN_DIM + 2 * KV_DIM:]
    for c0 in range(0, CONV_DIM, MXU_DIM):
        gate_c = _dot(h_ext, w_ref[:, n_head_cols + c0:n_head_cols + c0 + MXU_DIM])
        conv_in = _dot(h_ext, w_ref[:, n_head_cols + CONV_DIM + c0:n_head_cols + CONV_DIM + c0 + MXU_DIM])
        conv = _dwconv3_ext(gate_c * conv_in, cw_ref[:, c0:c0 + MXU_DIM])
        conv_out[:, c0:c0 + MXU_DIM] = (gate_b[:, c0:c0 + MXU_DIM] * conv).astype(conv_out.dtype)


def _inproj(x2d, g_all, g_row, w_in, layer, qg, kg, bd, c_tab, su_tab, sd_tab, sconv_w, *, seq):
    n_rows, d = x2d.shape
    tm = TM_PROJ
    tiles_per_seq = seq // tm
    main, prev, nxt = _halo_specs(tm, n_rows, d)
    tab = pl.BlockSpec((tm, LANES), lambda i: (i % tiles_per_seq, 0))
    k_spec = pl.BlockSpec((N_KV_HEADS, tm, LANES), lambda i: (0, i, 0))
    k_shape = jax.ShapeDtypeStruct((N_KV_HEADS, n_rows, LANES), BF16)
    q_scale = HEAD_DIM ** -0.5 * math.log2(math.e)
    return pl.pallas_call(
        functools.partial(_inproj_kernel, tm=tm, tiles_per_seq=tiles_per_seq, q_scale=q_scale),
        grid=(n_rows // tm,),
        in_specs=[main, prev, nxt, _layer_spec(g_all, g_row), _layer_spec(w_in, layer),
                  _const_spec(qg.shape), _const_spec(kg.shape), _const_spec(bd.shape),
                  tab, tab, tab, _layer_spec(sconv_w, layer)],
        out_specs=[pl.BlockSpec((ATTN_DIM, tm), lambda i: (0, i)),
                   k_spec, k_spec,
                   pl.BlockSpec((KV_DIM, tm), lambda i: (0, i)),
                   pl.BlockSpec((tm, CONV_DIM), lambda i: (i, 0))],
        out_shape=[jax.ShapeDtypeStruct((ATTN_DIM, n_rows), BF16),
                   k_shape, k_shape,
                   jax.ShapeDtypeStruct((KV_DIM, n_rows), BF16),
                   jax.ShapeDtypeStruct((n_rows, CONV_DIM), BF16)],
        compiler_params=pltpu.CompilerParams(
            dimension_semantics=("parallel",), vmem_limit_bytes=VMEM_LIMIT),
        name="inproj",
    )(x2d, x2d, x2d, g_all, w_in, qg, kg, bd, c_tab, su_tab, sd_tab, sconv_w)


def _attn_kernel(qt_ref, ka_ref, kb_ref, vt_ref, o_ref):
    seq = vt_ref.shape[1]
    tq = qt_ref.shape[1]
    k_refs = (ka_ref, kb_ref, ka_ref, kb_ref)
    q_cols = [slice(c0, c0 + ATTN_Q_BLOCK) for c0 in range(0, tq, ATTN_Q_BLOCK)]
    units = [(g, k0, qi) for g in range(Q_GROUP) for k0 in range(0, seq, ATTN_KEY_BLOCK)
             for qi in range(len(q_cols))]

    def scores(u):
        g, k0, qi = units[u]
        slab = g // 2
        return _dot(k_refs[g][k0:k0 + ATTN_KEY_BLOCK, :],
                    qt_ref[slab * LANES:(slab + 1) * LANES, q_cols[qi]])

    pending = [scores(u) for u in range(min(ATTN_LOOKAHEAD, len(units)))]
    ones = jnp.ones((BF16_ROWS, ATTN_KEY_BLOCK), BF16)
    outs = [[None] * len(q_cols) for _ in range(Q_GROUP)]
    m = [None] * len(q_cols)
    acc = [None] * len(q_cols)
    for u, (g, k0, qi) in enumerate(units):
        st = pending.pop(0)
        if u + ATTN_LOOKAHEAD < len(units):
            pending.append(scores(u + ATTN_LOOKAHEAD))
        vt1 = jnp.concatenate([vt_ref[:, k0:k0 + ATTN_KEY_BLOCK], ones], axis=0)
        blk_max = jnp.max(st, axis=0, keepdims=True)
        if k0 == 0:
            m[qi] = blk_max
            acc[qi] = _dot(vt1, jnp.exp2(st - blk_max).astype(BF16))
        else:
            m_new = jnp.maximum(m[qi], blk_max)
            acc[qi] = (jnp.exp2(m[qi] - m_new) * acc[qi]
                       + _dot(vt1, jnp.exp2(st - m_new).astype(BF16)))
            m[qi] = m_new
        if k0 + ATTN_KEY_BLOCK == seq:
            outs[g][qi] = acc[qi][:HEAD_DIM] * (1.0 / acc[qi][HEAD_DIM:HEAD_DIM + 1])
    o_t = jnp.concatenate([jnp.concatenate(row, axis=1) for row in outs], axis=0)
    o_ref[...] = o_t.T.astype(o_ref.dtype)


def _attention(qt, ka, kb, vt, *, batch, seq):
    tq = TQ_ATTN
    q_tiles = seq // tq
    group_cols = Q_GROUP * HEAD_DIM
    k_spec = pl.BlockSpec((None, seq, LANES), lambda b, h, i: (h, b, 0))
    return pl.pallas_call(
        _attn_kernel,
        grid=(batch, N_KV_HEADS, q_tiles),
        in_specs=[pl.BlockSpec((group_cols, tq), lambda b, h, i: (h, b * q_tiles + i)),
                  k_spec, k_spec,
                  pl.BlockSpec((HEAD_DIM, seq), lambda b, h, i: (h, b))],
        out_specs=pl.BlockSpec((tq, group_cols), lambda b, h, i: (b * q_tiles + i, h)),
        out_shape=jax.ShapeDtypeStruct((qt.shape[1], qt.shape[0]), BF16),
        compiler_params=pltpu.CompilerParams(
            dimension_semantics=("parallel", "parallel", "arbitrary"),
            vmem_limit_bytes=VMEM_LIMIT),
        name="attention",
    )(qt, ka, kb, vt)


def _halo16_specs(tm, n_rows, cols):
    per = tm // BF16_ROWS
    last = n_rows // BF16_ROWS - 1
    prev = pl.BlockSpec((BF16_ROWS, cols), lambda i: (jnp.maximum(i * per - 1, 0), 0))
    nxt = pl.BlockSpec((BF16_ROWS, cols), lambda i: (jnp.minimum((i + 1) * per, last), 0))
    return prev, nxt


def _ffn_core(x1_ext, g2, wup_ref, cw_ref, wdn_ref, g3, o_ref, act_ref, x1_ref,
              *, tiles_per_seq, chunk):
    tm, d_ff = act_ref.shape
    pos = pl.program_id(0) % tiles_per_seq
    x1_ref[...] = x1_ext[:tm]
    x1p = jnp.where(pos != 0, x1_ext[tm:tm + HALO], 0.0)
    x1n = jnp.where(pos != tiles_per_seq - 1, x1_ext[tm + HALO:], 0.0)
    h_ext = _rmsnorm(jnp.concatenate([x1_ref[...], x1p, x1n], axis=0), g2).astype(BF16)

    def up_branch(c0):
        p = _dot(h_ext, wup_ref[:, c0:c0 + chunk])
        return _dwconv3_ext(p, cw_ref[:, c0:c0 + chunk])

    def down_accumulate(c0, c1):
        y = _dot(act_ref[:, c0:c1], wdn_ref[c0:c1, :])
        if c0 == 0:
            o_ref[...] = y
        else:
            o_ref[...] += y

    done = []
    for c0 in range(0, d_ff, chunk):
        u = up_branch(c0)
        gt = up_branch(d_ff + c0)
        if len(done) == 2:
            down_accumulate(done[0], done[1] + chunk)
            done = []
        act = jax.nn.gelu(gt, approximate=True) * u
        act_ref[:, c0:c0 + chunk] = act.astype(act_ref.dtype)
        done.append(c0)
    down_accumulate(done[0], d_ff)
    o_ref[...] = x1_ref[...] + _rmsnorm(o_ref[...], g3)


def _tail_even_kernel(a_ref, ap_ref, an_ref, c_ref, cp_ref, cn_ref, x_ref, xp_ref, xn_ref,
                      wa_ref, wc_ref, g1_ref, g2_ref, wup_ref, cw_ref, wdn_ref, g3_ref,
                      o_ref, act_ref, x1_ref, *, tiles_per_seq, chunk):
    def ext(m_ref, p_ref, n_ref):
        halo = jnp.concatenate([p_ref[...].astype(F32)[HALO:], n_ref[...].astype(F32)[:HALO]], axis=0)
        return jnp.concatenate([m_ref[...], halo.astype(m_ref.dtype)], axis=0)

    a_ext, c_ext = ext(a_ref, ap_ref, an_ref), ext(c_ref, cp_ref, cn_ref)
    x_ext = jnp.concatenate([x_ref[...], xp_ref[HALO:], xn_ref[:HALO]], axis=0)
    split = x_ref.shape[0] // 2
    parts = []
    for r in (slice(0, split), slice(split, x_ext.shape[0])):
        mix = _dot(a_ext[r], wa_ref[...]) + _dot(c_ext[r], wc_ref[...])
        parts.append(x_ext[r] + _rmsnorm(mix, g1_ref[...]))
    x1_ext = jnp.concatenate(parts, axis=0)
    _ffn_core(x1_ext, g2_ref[...], wup_ref, cw_ref, wdn_ref, g3_ref[...], o_ref, act_ref, x1_ref,
              tiles_per_seq=tiles_per_seq, chunk=chunk)


def _tail_odd_kernel(x_ref, xp_ref, xn_ref, g0_ref, pw_ref, ps_ref,
                     g1_ref, g2_ref, wup_ref, cw_ref, wdn_ref, g3_ref,
                     o_ref, act_ref, x1_ref, *, tiles_per_seq, chunk, seq):
    tm, d = x_ref.shape
    pos = pl.program_id(0) % tiles_per_seq
    xp = jnp.where(pos != 0, xp_ref[...], 0.0)
    xn = jnp.where(pos != tiles_per_seq - 1, xn_ref[...], 0.0)
    x_wide = jnp.concatenate([xp, x_ref[...], xn], axis=0)
    h_wide = _rmsnorm(x_wide, g0_ref[...])
    n_wide = tm + 2 * BF16_ROWS
    n_ext = tm + 2 * HALO
    group = d // len(POOL_WINDOWS)

    def reorder(a):
        return jnp.concatenate([a[HALO:HALO + tm], a[:HALO], a[HALO + tm:]], axis=0)

    t = pos * tm - HALO + lax.broadcasted_iota(jnp.int32, (n_ext, LANES), 0)
    outs = []
    for gi, w in enumerate(POOL_WINDOWS):
        hg = h_wide[:, gi * group:(gi + 1) * group]
        acc, span = hg, 1
        while span < w:
            acc = acc + pltpu.roll(acc, span, 0)
            span *= 2
        ahead = w // 2 - 1
        if ahead:
            acc = pltpu.roll(acc, n_wide - ahead, 0)
        win = acc[HALO:HALO + n_ext]
        cnt = jnp.minimum(t + w // 2, seq) - jnp.maximum(t - w // 2, 0)
        inv = 1.0 / jnp.maximum(cnt, 1).astype(F32)
        inv = jnp.concatenate([inv] * (group // LANES), axis=1)
        pooled = win * inv - hg[HALO:HALO + n_ext]
        outs.append(_dot(reorder(pooled).astype(BF16), pw_ref[gi]))
    mix = jnp.concatenate(outs, axis=1) * ps_ref[...]
    x1_ext = reorder(x_wide[HALO:HALO + n_ext]) + _rmsnorm(mix, g1_ref[...])
    _ffn_core(x1_ext, g2_ref[...], wup_ref, cw_ref, wdn_ref, g3_ref[...], o_ref, act_ref, x1_ref,
              tiles_per_seq=tiles_per_seq, chunk=chunk)


def _tail_call(body, row_inputs, row_specs, const_inputs, const_specs, x2d, d_ff, name):
    n_rows, d = x2d.shape
    tm = TM_FFN
    return pl.pallas_call(
        body,
        grid=(n_rows // tm,),
        in_specs=list(row_specs) + list(const_specs),
        out_specs=pl.BlockSpec((tm, d), lambda i: (i, 0)),
        out_shape=jax.ShapeDtypeStruct(x2d.shape, x2d.dtype),
        scratch_shapes=[pltpu.VMEM((tm, d_ff), BF16), pltpu.VMEM((tm, d), F32)],
        compiler_params=pltpu.CompilerParams(
            dimension_semantics=("parallel",), vmem_limit_bytes=VMEM_LIMIT),
        name=name,
    )(*row_inputs, *const_inputs)


def _tail_even(attn, conv, x2d, w_out, e, g_all, g_row, w_up, conv_w, w_down, layer, *, seq):
    n_rows, d = x2d.shape
    tm = TM_FFN
    row = lambda cols: pl.BlockSpec((tm, cols), lambda i: (i, 0))
    a_prev, a_next = _halo16_specs(tm, n_rows, ATTN_DIM)
    c_prev, c_next = _halo16_specs(tm, n_rows, CONV_DIM)
    x_prev, x_next = _halo16_specs(tm, n_rows, d)
    body = functools.partial(_tail_even_kernel, tiles_per_seq=seq // tm, chunk=FFN_CHUNK)
    return _tail_call(
        body,
        (attn, attn, attn, conv, conv, conv, x2d, x2d, x2d),
        (row(ATTN_DIM), a_prev, a_next, row(CONV_DIM), c_prev, c_next, row(d), x_prev, x_next),
        (w_out, w_out, g_all, g_all, w_up, conv_w, w_down, g_all),
        (_layer_spec(w_out, e, (ATTN_DIM, d), (0, 0)),
         _layer_spec(w_out, e, (CONV_DIM, d), (ATTN_DIM // CONV_DIM, 0)),
         _layer_spec(g_all, g_row + 1), _layer_spec(g_all, g_row + 2),
         _layer_spec(w_up, layer), _layer_spec(conv_w, layer), _layer_spec(w_down, layer),
         _layer_spec(g_all, g_row + 3)),
        x2d, w_down.shape[1], "tail_even")


def _tail_odd(x2d, pool_w, pool_scale, o, g_all, g_row, w_up, conv_w, w_down, layer, *, seq):
    n_rows, d = x2d.shape
    tm = TM_FFN
    x_prev, x_next = _halo16_specs(tm, n_rows, d)
    body = functools.partial(_tail_odd_kernel, tiles_per_seq=seq // tm, chunk=FFN_CHUNK, seq=seq)
    return _tail_call(
        body,
        (x2d, x2d, x2d),
        (pl.BlockSpec((tm, d), lambda i: (i, 0)), x_prev, x_next),
        (g_all, pool_w, pool_scale, g_all, g_all, w_up, conv_w, w_down, g_all),
        (_layer_spec(g_all, g_row), _layer_spec(pool_w, o), _layer_spec(pool_scale, o),
         _layer_spec(g_all, g_row + 1), _layer_spec(g_all, g_row + 2),
         _layer_spec(w_up, layer), _layer_spec(conv_w, layer), _layer_spec(w_down, layer),
         _layer_spec(g_all, g_row + 3)),
        x2d, w_down.shape[1], "tail_odd")


def _rope_tables(seq):
    rows = seq // GRID_W
    row = jnp.repeat(jnp.arange(rows, dtype=F32), GRID_W)
    col = jnp.tile(jnp.arange(GRID_W, dtype=F32), rows)
    inv = ROPE_THETA ** (-jnp.arange(0, ROPE_AXIS_DIM, 2, dtype=F32) / ROPE_AXIS_DIM)
    ang_r = row[:, None] * inv[None, :]
    ang_c = col[:, None] * inv[None, :]
    cos_r, sin_r, cos_c, sin_c = jnp.cos(ang_r), jnp.sin(ang_r), jnp.cos(ang_c), jnp.sin(ang_c)
    zero = jnp.zeros_like(sin_r)
    c = jnp.concatenate([cos_r, cos_r, cos_c, cos_c], axis=1)
    su = jnp.concatenate([zero, sin_r, zero, sin_c], axis=1)
    sd = jnp.concatenate([-sin_r, zero, -sin_c, zero], axis=1)
    reps = LANES // HEAD_DIM
    return tuple(jnp.tile(t, (1, reps)) for t in (c, su, sd))


def kernel(x, norm_g, mix_w_in, q_norm_g, k_norm_g, sconv_w, mix_w_out, pool_w, pool_scale,
           ffn_w_up, ffn_conv_w, ffn_w_down):
    batch, seq, d = x.shape
    depth, norms_per_layer = norm_g.shape[:2]
    assert seq % TM_PROJ == 0 and seq % TM_FFN == 0 and seq % TQ_ATTN == 0
    assert ffn_w_down.shape[1] % FFN_CHUNK == 0

    c_tab, su_tab, sd_tab = _rope_tables(seq)
    lane_head = jnp.arange(MXU_DIM) // HEAD_DIM
    bd = jnp.where(lane_head[:, None] == lane_head[None, :], 1.0 / HEAD_DIM, 0.0).astype(BF16)

    w_in, w_out, pool_w_bf = mix_w_in.astype(BF16), mix_w_out.astype(BF16), pool_w.astype(BF16)
    w_up, w_down = ffn_w_up.astype(BF16), ffn_w_down.astype(BF16)
    g_all = norm_g.reshape(depth * norms_per_layer, 1, d)
    pool_scale3 = pool_scale[:, None, :]

    x2d = x.reshape(batch * seq, d)
    for i in range(depth):
        g_row = i * norms_per_layer
        if i % 2 == 0:
            e = i // 2
            qg = jnp.tile(q_norm_g[e], N_Q_HEADS)[None, :]
            kg = jnp.tile(k_norm_g[e], N_KV_HEADS)[None, :]
            qt, ka, kb, vt, conv = _inproj(
                x2d, g_all, g_row, w_in, e, qg, kg, bd, c_tab, su_tab, sd_tab, sconv_w, seq=seq)
            attn = _attention(qt, ka, kb, vt, batch=batch, seq=seq)
            x2d = _tail_even(attn, conv, x2d, w_out, e, g_all, g_row, w_up, ffn_conv_w, w_down, i,
                             seq=seq)
        else:
            x2d = _tail_odd(x2d, pool_w_bf, pool_scale3, i // 2, g_all, g_row, w_up, ffn_conv_w,
                            w_down, i, seq=seq)
    return x2d.reshape(batch, seq, d)
```
